```python
import math
import jax, jax.numpy as jnp
from jax import lax
import numpy as np

D_MODEL = 1024
BATCH = 2
SEQ = 16384
DEPTH = 2

N_MIXERS = 2
ATTN_HEADS = 8
ATTN_HEAD_DIM = D_MODEL // ATTN_HEADS // 2
ATTN_V_DIM = 2 * ATTN_HEAD_DIM
Q_BLOCK = 128
SGU_WIDTH = 2 * D_MODEL
SGU_GROUPS = 8
SGU_GROUP_DIM = SGU_WIDTH // SGU_GROUPS
CHUNK = 128
FFN_DIM = 2816
CONV_WIDTH = 3
NORM_EPS = 1e-6
LN_EPS = 1e-5

N_ATTN_LAYERS = (DEPTH + N_MIXERS - 1) // N_MIXERS
N_SGU_LAYERS = DEPTH // N_MIXERS

kernel_name = "hybrid_diffattn_chunksgu_convffn"


def _alibi_slopes(n_heads):
    return jnp.asarray(2.0 ** (-8.0 * np.arange(1, n_heads + 1) / n_heads), dtype=jnp.float32)


def rms_norm(x, gain):
    xf = x.astype(jnp.float32)
    y = xf * lax.rsqrt(jnp.mean(xf * xf, axis=-1, keepdims=True) + NORM_EPS) * gain.astype(jnp.float32)
    return y.astype(x.dtype)


def diff_attention(h, w_qkv, lq1, lk1, lq2, lk2, subln, w_o, layer_idx):
    B, S, _ = h.shape
    H, DH, VD = ATTN_HEADS, ATTN_HEAD_DIM, ATTN_V_DIM
    qkv = h @ w_qkv
    q, k, v = jnp.split(qkv, [H * 2 * DH, 2 * H * 2 * DH], axis=-1)
    q = q.reshape(B, S, H, 2, DH).transpose(0, 2, 3, 1, 4)
    k = k.reshape(B, S, H, 2, DH).transpose(0, 2, 3, 1, 4)
    v = v.reshape(B, S, H, VD).transpose(0, 2, 1, 3)

    lam_init = 0.8 - 0.6 * math.exp(-0.3 * layer_idx)
    f32 = jnp.float32
    lam = (jnp.exp(jnp.sum(lq1.astype(f32) * lk1.astype(f32)))
           - jnp.exp(jnp.sum(lq2.astype(f32) * lk2.astype(f32))) + lam_init)
    slopes = _alibi_slopes(H)
    s_pos = jnp.arange(S)
    n_qb = S // Q_BLOCK
    q_blocks = q.reshape(B, H, 2, n_qb, Q_BLOCK, DH).transpose(3, 0, 1, 2, 4, 5)
    scale = DH ** -0.5

    def one_block(args):
        qb, bi = args
        t_pos = bi * Q_BLOCK + jnp.arange(Q_BLOCK)
        dist = (t_pos[:, None] - s_pos[None, :]).astype(f32)
        bias = jnp.where(dist[None] >= 0, -slopes[:, None, None] * dist[None], -jnp.inf)
        scores = jnp.einsum('bhmqd,bhmkd->bhmqk', qb, k).astype(f32) * scale + bias[None, :, None]
        p = jax.nn.softmax(scores, axis=-1)
        p_diff = p[:, :, 0] - lam * p[:, :, 1]
        o = jnp.einsum('bhqk,bhkd->bhqd', p_diff.astype(v.dtype), v)
        return rms_norm(o, subln) * (1.0 - lam_init)

    out = lax.map(one_block, (q_blocks, jnp.arange(n_qb)))
    out = out.transpose(1, 0, 3, 2, 4).reshape(B, S, H * VD)
    return out @ w_o


def chunked_sgu(h, w_in, ln_g, ln_b, w_s, b_s, w_out):
    B, S, _ = h.shape
    z = jax.nn.gelu(h @ w_in)
    u, v = jnp.split(z, 2, axis=-1)
    vf = v.astype(jnp.float32)
    mu = jnp.mean(vf, axis=-1, keepdims=True)
    var = jnp.mean(jnp.square(vf - mu), axis=-1, keepdims=True)
    v = ((vf - mu) * lax.rsqrt(var + LN_EPS) * ln_g.astype(jnp.float32)
         + ln_b.astype(jnp.float32)).astype(h.dtype)
    n_c = S // CHUNK
    v = v.reshape(B, n_c, CHUNK, SGU_GROUPS, SGU_GROUP_DIM)
    causal = jnp.tril(jnp.ones((CHUNK, CHUNK), dtype=bool))
    w = jnp.where(causal[None], w_s, jnp.zeros_like(w_s))
    s = jnp.einsum('gts,bcsgd->bctgd', w, v) + b_s.T[None, None, :, :, None]
    y = u * s.reshape(B, S, SGU_WIDTH)
    return y @ w_out


def conv_ffn(h, w_up, conv_w, conv_b, w_down):
    a, g = jnp.split(h @ w_up, 2, axis=-1)
    S = a.shape[1]
    a_pad = jnp.pad(a, ((0, 0), (CONV_WIDTH - 1, 0), (0, 0)))
    a_conv = conv_b
    for j in range(CONV_WIDTH):
        a_conv = a_conv + a_pad[:, j:j + S] * conv_w[j]
    return (jax.nn.gelu(a_conv) * g) @ w_down


def setup_inputs(seed: int = 0) -> dict:
    key = jax.random.key(seed)
    ks = iter(jax.random.split(key, 32))
    f32 = jnp.float32
    D, E, F, G, C, DH = D_MODEL, SGU_WIDTH, FFN_DIM, SGU_GROUPS, CHUNK, ATTN_HEAD_DIM
    NA, NG, L = N_ATTN_LAYERS, N_SGU_LAYERS, DEPTH

    def nrm(shape, scale):
        return jax.random.normal(next(ks), shape, f32) * scale

    def gain(shape):
        return 1.0 + nrm(shape, 0.05)

    return {
        "x": nrm((BATCH, SEQ, D), 1.0),
        "attn_w_qkv": nrm((NA, D, 3 * D), D ** -0.5),
        "attn_lambda_q1": nrm((NA, DH), 0.1),
        "attn_lambda_k1": nrm((NA, DH), 0.1),
        "attn_lambda_q2": nrm((NA, DH), 0.1),
        "attn_lambda_k2": nrm((NA, DH), 0.1),
        "attn_subln": gain((NA, ATTN_V_DIM)),
        "attn_w_o": nrm((NA, D, D), D ** -0.5),
        "sgu_w_in": nrm((NG, D, 2 * E), D ** -0.5),
        "sgu_ln_g": gain((NG, E)),
        "sgu_ln_b": nrm((NG, E), 0.02),
        "sgu_w_s": nrm((NG, G, C, C), C ** -0.5),
        "sgu_b_s": gain((NG, G, C)),
        "sgu_w_out": nrm((NG, E, D), E ** -0.5),
        "norm_mix_pre": gain((L, D)),
        "norm_mix_post": gain((L, D)),
        "norm_ffn_pre": gain((L, D)),
        "norm_ffn_post": gain((L, D)),
        "ffn_w_up": nrm((L, D, 2 * F), D ** -0.5),
        "ffn_conv_w": nrm((L, CONV_WIDTH, F), CONV_WIDTH ** -0.5),
        "ffn_conv_b": nrm((L, F), 0.02),
        "ffn_w_down": nrm((L, F, D), F ** -0.5),
    }


def reference(x, attn_w_qkv, attn_lambda_q1, attn_lambda_k1, attn_lambda_q2, attn_lambda_k2,
              attn_subln, attn_w_o, sgu_w_in, sgu_ln_g, sgu_ln_b, sgu_w_s, sgu_b_s, sgu_w_out,
              norm_mix_pre, norm_mix_post, norm_ffn_pre, norm_ffn_post,
              ffn_w_up, ffn_conv_w, ffn_conv_b, ffn_w_down):
    for i in range(DEPTH):
        j = i // N_MIXERS
        hn = rms_norm(x, norm_mix_pre[i])
        if i % N_MIXERS == 0:
            m = diff_attention(hn, attn_w_qkv[j], attn_lambda_q1[j], attn_lambda_k1[j],
                               attn_lambda_q2[j], attn_lambda_k2[j], attn_subln[j], attn_w_o[j], i)
        else:
            m = chunked_sgu(hn, sgu_w_in[j], sgu_ln_g[j], sgu_ln_b[j], sgu_w_s[j], sgu_b_s[j],
                            sgu_w_out[j])
        x = x + rms_norm(m, norm_mix_post[i])
        f = conv_ffn(rms_norm(x, norm_ffn_pre[i]), ffn_w_up[i], ffn_conv_w[i], ffn_conv_b[i],
                     ffn_w_down[i])
        x = x + rms_norm(f, norm_ffn_post[i])
    return x
```

```python
import functools
import math

import jax
import jax.numpy as jnp
from jax import lax
from jax.experimental import pallas as pl
from jax.experimental.pallas import tpu as pltpu

F32 = jnp.float32
BF16 = jnp.bfloat16

D_MODEL = 1024
ATTN_HEADS = 8
HEAD_DIM = 64
V_DIM = 2 * HEAD_DIM
SGU_WIDTH = 2 * D_MODEL
SGU_GROUPS = 8
SGU_GROUP_DIM = SGU_WIDTH // SGU_GROUPS
CHUNK = 128
FFN_DIM = 2816
NORM_EPS = 1e-6
LN_EPS = 1e-5
LAM_INIT = 0.8 - 0.6 * math.exp(-0.3 * 0)

MXU_DIM = 256
SUBLANES = 8
VMEM_LIMIT_BYTES = 56 * 1024 * 1024

QKV_ROWS = 512
ATTN_TQ = 256
ATTN_TK = 256
ROW_TILE = 512
FFN_CHUNK = MXU_DIM
MASK_VALUE = -1e30

_NT_DIMS = (((1,), (1,)), ((), ()))


def _rms(x, gain):
    return x * lax.rsqrt(jnp.mean(x * x, axis=-1, keepdims=True) + NORM_EPS) * gain


def _gelu_tanh(x):
    return x * (0.5 * (1.0 + jnp.tanh(0.7978845608028654 * (x + 0.044715 * (x * x * x)))))


def _qkv_kernel(x_ref, g_ref, wqt_ref, wk_ref, wvt_ref, qt_ref, k_ref, vt_ref):
    hn = _rms(x_ref[0], g_ref[...]).astype(BF16)
    k_ref[0] = jnp.dot(hn, wk_ref[...], preferred_element_type=F32).astype(BF16)
    qt_ref[0] = lax.dot_general(wqt_ref[...], hn, _NT_DIMS,
                                preferred_element_type=F32).astype(BF16)
    vt_ref[0] = lax.dot_general(wvt_ref[...], hn, _NT_DIMS,
                                preferred_element_type=F32).astype(BF16)


def _qkv_call(x, gain, wqt, wk, wvt):
    b, s, d = x.shape
    rows = QKV_ROWS
    const = lambda bi, i: (0, 0)
    return pl.pallas_call(
        _qkv_kernel,
        grid=(b, s // rows),
        in_specs=[
            pl.BlockSpec((1, rows, d), lambda bi, i: (bi, i, 0)),
            pl.BlockSpec((1, d), const),
            pl.BlockSpec((d, d), const),
            pl.BlockSpec((d, d), const),
            pl.BlockSpec((d, d), const),
        ],
        out_specs=[
            pl.BlockSpec((1, d, rows), lambda bi, i: (bi, 0, i)),
            pl.BlockSpec((1, rows, d), lambda bi, i: (bi, i, 0)),
            pl.BlockSpec((1, d, rows), lambda bi, i: (bi, 0, i)),
        ],
        out_shape=[
            jax.ShapeDtypeStruct((b, d, s), BF16),
            jax.ShapeDtypeStruct((b, s, d), BF16),
            jax.ShapeDtypeStruct((b, d, s), BF16),
        ],
        compiler_params=pltpu.CompilerParams(
            dimension_semantics=("parallel", "parallel"),
            vmem_limit_bytes=VMEM_LIMIT_BYTES),
        name="qkv_proj",
    )(x, gain, wqt, wk, wvt)


def _attn_kernel(shift_ref, qt_ref, k_ref, vt_ref, bias_ref, lq1_ref, lk1_ref, lq2_ref,
                 lk2_ref, subln_ref, o_ref, w_ref, m_ref, l_ref, acc_ref):
    tq, tk = ATTN_TQ, ATTN_TK
    head = pl.program_id(1)
    qi = pl.program_id(2)
    shift = shift_ref[head]

    qt = qt_ref[0]
    zeros = jnp.zeros((HEAD_DIM, tq), BF16)
    row = lax.broadcasted_iota(jnp.int32, (V_DIM, tq), 0)
    ones = jnp.where(row < 2, 1.0, 0.0).astype(BF16)
    w_ref[:, 0:tq] = jnp.concatenate([qt[0:HEAD_DIM], zeros, ones], axis=0)
    w_ref[:, tq:2 * tq] = jnp.concatenate([zeros, qt[HEAD_DIM:V_DIM], ones], axis=0)
    m_ref[...] = jnp.full(m_ref.shape, MASK_VALUE, F32)
    l_ref[...] = jnp.zeros(l_ref.shape, F32)
    acc_ref[...] = jnp.zeros(acc_ref.shape, F32)

    def key_block(j, causal):
        start = pl.multiple_of(j * tk, tk)
        k_aug = jnp.concatenate([k_ref[0, pl.ds(start, tk), :], bias_ref[0]], axis=1)
        s = jnp.dot(k_aug, w_ref[...], preferred_element_type=F32)
        vt = vt_ref[0, :, pl.ds(start, tk)]
        for mi in range(2):
            sm = s[:, mi * tq:(mi + 1) * tq]
            if causal:
                key_pos = lax.broadcasted_iota(jnp.int32, (tk, tq), 0)
                qry_pos = lax.broadcasted_iota(jnp.int32, (tk, tq), 1)
                sm = jnp.where(key_pos <= qry_pos, sm, MASK_VALUE)
            m_prev = m_ref[mi] - shift
            m_new = jnp.maximum(m_prev, jnp.max(sm, axis=0, keepdims=True))
            alpha = jnp.exp(m_prev - m_new)
            p = jnp.exp(sm - m_new)
            l_ref[mi] = alpha * l_ref[mi] + jnp.sum(p, axis=0, keepdims=True)
            acc_ref[mi] = alpha * acc_ref[mi] + jnp.dot(
                vt, p.astype(BF16), preferred_element_type=F32)
            m_ref[mi] = m_new

    def off_diagonal(j, carry):
        key_block(j, False)
        return carry

    lax.fori_loop(0, qi, off_diagonal, 0)
    key_block(qi, True)

    lam = (jnp.exp(jnp.sum(lq1_ref[...] * lk1_ref[...], keepdims=True))
           - jnp.exp(jnp.sum(lq2_ref[...] * lk2_ref[...], keepdims=True)) + LAM_INIT)
    ot = acc_ref[0] * (1.0 / l_ref[0]) - lam * (acc_ref[1] * (1.0 / l_ref[1]))
    o = ot.T
    o_ref[0] = (_rms(o, subln_ref[...]) * (1.0 - LAM_INIT)).astype(BF16)


def _alibi_tables():
    slopes = [2.0 ** (-8.0 * (h + 1) / ATTN_HEADS) for h in range(ATTN_HEADS)]
    pos = jnp.arange(ATTN_TK)
    low = (pos % MXU_DIM).astype(F32)
    high = (pos - pos % MXU_DIM).astype(F32)
    tile = jnp.zeros((ATTN_HEADS, ATTN_TK, V_DIM), F32)
    sl = jnp.asarray(slopes, F32)[:, None]
    tile = tile.at[:, :, 0].set(sl * low[None, :])
    tile = tile.at[:, :, 1].set(sl * high[None, :])
    shifts = jnp.asarray([s * ATTN_TK for s in slopes], F32)
    return tile.astype(BF16), shifts


def _attn_call(qt, k, vt, lq1, lk1, lq2, lk2, subln):
    b, s, d = k.shape
    tq, tk = ATTN_TQ, ATTN_TK
    bias, shifts = _alibi_tables()
    vec = lambda n: pl.BlockSpec((1, n), lambda bi, h, i: (0, 0))
    return pl.pallas_call(
        _attn_kernel,
        grid=(b, ATTN_HEADS, s // tq),
        in_specs=[
            pl.BlockSpec(memory_space=pltpu.SMEM),
            pl.BlockSpec((1, V_DIM, tq), lambda bi, h, i: (bi, h, i)),
            pl.BlockSpec((1, s, V_DIM), lambda bi, h, i: (bi, 0, h)),
            pl.BlockSpec((1, V_DIM, s), lambda bi, h, i: (bi, h, 0)),
            pl.BlockSpec((1, tk, V_DIM), lambda bi, h, i: (h, 0, 0)),
            vec(HEAD_DIM), vec(HEAD_DIM), vec(HEAD_DIM), vec(HEAD_DIM),
            vec(V_DIM),
        ],
        out_specs=pl.BlockSpec((1, tq, V_DIM), lambda bi, h, i: (bi, i, h)),
        out_shape=jax.ShapeDtypeStruct((b, s, d), BF16),
        scratch_shapes=[
            pltpu.VMEM((2 * V_DIM, 2 * tq), BF16),
            pltpu.VMEM((2, 1, tq), F32),
            pltpu.VMEM((2, 1, tq), F32),
            pltpu.VMEM((2, V_DIM, tq), F32),
        ],
        compiler_params=pltpu.CompilerParams(
            dimension_semantics=("parallel", "parallel", "arbitrary"),
            vmem_limit_bytes=VMEM_LIMIT_BYTES),
        name="diff_attn",
    )(shifts, qt, k, vt, bias, lq1, lk1, lq2, lk2, subln)


def _oproj_kernel(x_ref, a_ref, w_ref, g_ref, o_ref):
    m = jnp.dot(a_ref[...], w_ref[...], preferred_element_type=F32)
    o_ref[...] = x_ref[...] + _rms(m, g_ref[...])


def _oproj_call(x2d, a2d, w, gain):
    t, d = x2d.shape
    rows = ROW_TILE
    return pl.pallas_call(
        _oproj_kernel,
        grid=(t // rows,),
        in_specs=[
            pl.BlockSpec((rows, d), lambda i: (i, 0)),
            pl.BlockSpec((rows, d), lambda i: (i, 0)),
            pl.BlockSpec((d, d), lambda i: (0, 0)),
            pl.BlockSpec((1, d), lambda i: (0, 0)),
        ],
        out_specs=pl.BlockSpec((rows, d), lambda i: (i, 0)),
        out_shape=jax.ShapeDtypeStruct((t, d), F32),
        compiler_params=pltpu.CompilerParams(
            dimension_semantics=("parallel",),
            vmem_limit_bytes=VMEM_LIMIT_BYTES),
        name="attn_oproj",
    )(x2d, a2d, w, gain)


def _ffn_kernel(x_ref, gpre_ref, wa_ref, wg_ref, cw_ref, wd_ref, gpost_ref, o_ref,
                hn_ref, abuf_ref, carry_ref, act_ref, *, tiles_per_seq):
    rows = x_ref.shape[0]
    seq_start = (pl.program_id(0) % tiles_per_seq) == 0
    x = x_ref[...]
    hn_ref[...] = _rms(x, gpre_ref[...]).astype(BF16)
    for c in range(FFN_DIM // FFN_CHUNK):
        cols = slice(c * FFN_CHUNK, (c + 1) * FFN_CHUNK)
        a = jnp.dot(hn_ref[...], wa_ref[:, cols], preferred_element_type=F32)
        abuf_ref[0:SUBLANES, :] = jnp.where(seq_start, 0.0, carry_ref[c])
        abuf_ref[SUBLANES:SUBLANES + rows, :] = a
        carry_ref[c] = a[rows - SUBLANES:rows, :]
        cw = cw_ref[:, cols]
        a_conv = (cw[3:4] + abuf_ref[SUBLANES - 2:SUBLANES - 2 + rows, :] * cw[0:1]
                  + abuf_ref[SUBLANES - 1:SUBLANES - 1 + rows, :] * cw[1:2] + a * cw[2:3])
        g = jnp.dot(hn_ref[...], wg_ref[:, cols], preferred_element_type=F32)
        act_ref[:, cols] = (_gelu_tanh(a_conv) * g).astype(BF16)
    f = jnp.dot(act_ref[...], wd_ref[...], preferred_element_type=F32)
    o_ref[...] = x + _rms(f, gpost_ref[...])


def _ffn_call(x2d, seq_len, gpre, w_up, conv_tab, w_down, gpost):
    t, d = x2d.shape
    rows = ROW_TILE
    const = lambda i: (0, 0)
    resident = pl.Buffered(1)
    return pl.pallas_call(
        functools.partial(_ffn_kernel, tiles_per_seq=seq_len // rows),
        grid=(t // rows,),
        in_specs=[
            pl.BlockSpec((rows, d), lambda i: (i, 0)),
            pl.BlockSpec((1, d), const),
            pl.BlockSpec((d, FFN_DIM), lambda i: (0, 0), pipeline_mode=resident),
            pl.BlockSpec((d, FFN_DIM), lambda i: (0, 1), pipeline_mode=resident),
            pl.BlockSpec((SUBLANES, FFN_DIM), const),
            pl.BlockSpec((FFN_DIM, d), const, pipeline_mode=resident),
            pl.BlockSpec((1, d), const),
        ],
        out_specs=pl.BlockSpec((rows, d), lambda i: (i, 0)),
        out_shape=jax.ShapeDtypeStruct((t, d), F32),
        scratch_shapes=[
            pltpu.VMEM((rows, d), BF16),
            pltpu.VMEM((rows + SUBLANES, FFN_CHUNK), F32),
            pltpu.VMEM((FFN_DIM // FFN_CHUNK, SUBLANES, FFN_CHUNK), F32),
            pltpu.VMEM((rows, FFN_DIM), BF16),
        ],
        compiler_params=pltpu.CompilerParams(
            dimension_semantics=("arbitrary",),
            vmem_limit_bytes=VMEM_LIMIT_BYTES),
        name="conv_ffn",
    )(x2d, gpre, w_up, w_up, conv_tab, w_down, gpost)


def _sgu_kernel(x_ref, gpre_ref, win_ref, lng_ref, lnb_ref, ws_ref, bst_ref, wout_ref,
                gpost_ref, o_ref, hn_ref, v_ref, vn_ref, y_ref):
    rows = x_ref.shape[0]
    gd = SGU_GROUP_DIM
    x = x_ref[...]
    hn_ref[...] = _rms(x, gpre_ref[...]).astype(BF16)
    for g in range(SGU_GROUPS):
        cols = slice(SGU_WIDTH + g * gd, SGU_WIDTH + (g + 1) * gd)
        v_ref[:, g * gd:(g + 1) * gd] = _gelu_tanh(
            jnp.dot(hn_ref[...], win_ref[:, cols], preferred_element_type=F32))
    v = v_ref[...]
    mu = jnp.mean(v, axis=-1, keepdims=True)
    vc = v - mu
    var = jnp.mean(vc * vc, axis=-1, keepdims=True)
    vn_ref[...] = (vc * lax.rsqrt(var + LN_EPS) * lng_ref[...] + lnb_ref[...]).astype(BF16)
    t_pos = lax.broadcasted_iota(jnp.int32, (CHUNK, CHUNK), 0)
    s_pos = lax.broadcasted_iota(jnp.int32, (CHUNK, CHUNK), 1)
    for g in range(SGU_GROUPS):
        cols = slice(g * gd, (g + 1) * gd)
        u = _gelu_tanh(jnp.dot(hn_ref[...], win_ref[:, cols], preferred_element_type=F32))
        w = jnp.where(s_pos <= t_pos, ws_ref[g], 0.0).astype(BF16)
        bias = bst_ref[:, g:g + 1]
        for c in range(rows // CHUNK):
            r = slice(c * CHUNK, (c + 1) * CHUNK)
            s = jnp.dot(w, vn_ref[r, cols], preferred_element_type=F32) + bias
            y_ref[r, cols] = (u[r] * s).astype(BF16)
    out = jnp.dot(y_ref[...], wout_ref[...], preferred_element_type=F32)
    o_ref[...] = x + _rms(out, gpost_ref[...])


def _sgu_call(x2d, gpre, w_in, ln_g, ln_b, w_s, b_st, w_out, gpost):
    t, d = x2d.shape
    rows = ROW_TILE
    const = lambda i: (0, 0)
    resident = pl.Buffered(1)
    return pl.pallas_call(
        _sgu_kernel,
        grid=(t // rows,),
        in_specs=[
            pl.BlockSpec((rows, d), lambda i: (i, 0)),
            pl.BlockSpec((1, d), const),
            pl.BlockSpec((d, 2 * SGU_WIDTH), const, pipeline_mode=resident),
            pl.BlockSpec((1, SGU_WIDTH), const),
            pl.BlockSpec((1, SGU_WIDTH), const),
            pl.BlockSpec((SGU_GROUPS, CHUNK, CHUNK), lambda i: (0, 0, 0)),
            pl.BlockSpec((CHUNK, SGU_GROUPS), const),
            pl.BlockSpec((SGU_WIDTH, d), const, pipeline_mode=resident),
            pl.BlockSpec((1, d), const),
        ],
        out_specs=pl.BlockSpec((rows, d), lambda i: (i, 0)),
        out_shape=jax.ShapeDtypeStruct((t, d), F32),
        scratch_shapes=[
            pltpu.VMEM((rows, d), BF16),
            pltpu.VMEM((rows, SGU_WIDTH), F32),
            pltpu.VMEM((rows, SGU_WIDTH), BF16),
            pltpu.VMEM((rows, SGU_WIDTH), BF16),
        ],
        compiler_params=pltpu.CompilerParams(
            dimension_semantics=("parallel",),
            vmem_limit_bytes=VMEM_LIMIT_BYTES),
        name="chunked_sgu",
    )(x2d, gpre, w_in, ln_g, ln_b, w_s, b_st, w_out, gpost)


def _conv_table(conv_w, conv_b):
    pad = jnp.zeros((SUBLANES - 4, FFN_DIM), F32)
    return jnp.concatenate([conv_w, conv_b[None, :], pad], axis=0)


def _row(v):
    return v.reshape(1, -1)


def kernel(x, attn_w_qkv, attn_lambda_q1, attn_lambda_k1, attn_lambda_q2, attn_lambda_k2, attn_subln, attn_w_o, sgu_w_in, sgu_ln_g, sgu_ln_b, sgu_w_s, sgu_b_s, sgu_w_out, norm_mix_pre, norm_mix_post, norm_ffn_pre, norm_ffn_post, ffn_w_up, ffn_conv_w, ffn_conv_b, ffn_w_down):
    b, s, d = x.shape
    assert d == D_MODEL and s % max(QKV_ROWS, ROW_TILE, ATTN_TQ) == 0 and ATTN_TQ == ATTN_TK

    w_qkv = attn_w_qkv[0]
    wqt = (w_qkv[:, :d].T * (HEAD_DIM ** -0.5)).astype(BF16)
    wk = w_qkv[:, d:2 * d].astype(BF16)
    wvt = w_qkv[:, 2 * d:].T.astype(BF16)
    qt, k, vt = _qkv_call(x, _row(norm_mix_pre[0]), wqt, wk, wvt)
    attn = _attn_call(qt, k, vt, _row(attn_lambda_q1[0]), _row(attn_lambda_k1[0]),
                      _row(attn_lambda_q2[0]), _row(attn_lambda_k2[0]), _row(attn_subln[0]))
    x2d = x.reshape(b * s, d)
    x2d = _oproj_call(x2d, attn.reshape(b * s, d), attn_w_o[0].astype(BF16),
                      _row(norm_mix_post[0]))
    x2d = _ffn_call(x2d, s, _row(norm_ffn_pre[0]), ffn_w_up[0].astype(BF16),
                    _conv_table(ffn_conv_w[0], ffn_conv_b[0]), ffn_w_down[0].astype(BF16),
                    _row(norm_ffn_post[0]))

    x2d = _sgu_call(x2d, _row(norm_mix_pre[1]), sgu_w_in[0].astype(BF16), _row(sgu_ln_g[0]),
                    _row(sgu_ln_b[0]), sgu_w_s[0], sgu_b_s[0].T, sgu_w_out[0].astype(BF16),
                    _row(norm_mix_post[1]))
    x2d = _ffn_call(x2d, s, _row(norm_ffn_pre[1]), ffn_w_up[1].astype(BF16),
                    _conv_table(ffn_conv_w[1], ffn_conv_b[1]), ffn_w_down[1].astype(BF16),
                    _row(norm_ffn_post[1]))
    return x2d.reshape(b, s, d)
```

```python
import functools
import math

import jax
import jax.numpy as jnp
from jax import lax
from jax.experimental import pallas as pl
from jax.experimental.pallas import tpu as pltpu

F32 = jnp.float32
BF16 = jnp.bfloat16

D_MODEL = 1024
ATTN_HEADS = 8
HEAD_DIM = 64
V_DIM = 2 * HEAD_DIM
SGU_WIDTH = 2 * D_MODEL
SGU_GROUPS = 8
SGU_GROUP_DIM = SGU_WIDTH // SGU_GROUPS
CHUNK = 128
FFN_DIM = 2816
NORM_EPS = 1e-6
LN_EPS = 1e-5
LAM_INIT = 0.8 - 0.6 * math.exp(-0.3 * 0)

MXU_DIM = 256
SUBLANES = 8
V_PAD = 16
V_AUG = V_DIM + V_PAD
LOG2E = math.log2(math.e)
VMEM_LIMIT_BYTES = 56 * 1024 * 1024

QKV_ROWS = 512
ATTN_TQ = 256
ATTN_TK = 256
ROW_TILE = 512
FFN_CHUNK = MXU_DIM
MASK_VALUE = -1e30

_NT_DIMS = (((1,), (1,)), ((), ()))


def _rms(x, gain):
    return x * lax.rsqrt(jnp.mean(x * x, axis=-1, keepdims=True) + NORM_EPS) * gain


def _gelu_tanh(x):
    return x * (0.5 * (1.0 + jnp.tanh(0.7978845608028654 * (x + 0.044715 * (x * x * x)))))


def _qkv_kernel(x_ref, g_ref, wqt_ref, wk_ref, wvt_ref, qt_ref, k_ref, vt_ref):
    rows = x_ref.shape[1]
    hn = _rms(x_ref[0], g_ref[...]).astype(BF16)
    k_ref[0] = jnp.dot(hn, wk_ref[...], preferred_element_type=F32).astype(BF16)
    qt_ref[0] = lax.dot_general(wqt_ref[...], hn, _NT_DIMS,
                                preferred_element_type=F32).astype(BF16)
    vt = lax.dot_general(wvt_ref[...], hn, _NT_DIMS,
                         preferred_element_type=F32).astype(BF16)
    pad_row = lax.broadcasted_iota(jnp.int32, (V_PAD, rows), 0)
    ones_rows = jnp.where(pad_row == 0, 1.0, 0.0).astype(BF16)
    for h in range(ATTN_HEADS):
        vt_ref[0, h * V_AUG:h * V_AUG + V_DIM, :] = vt[h * V_DIM:(h + 1) * V_DIM]
        vt_ref[0, h * V_AUG + V_DIM:(h + 1) * V_AUG, :] = ones_rows


def _qkv_call(x, gain, wqt, wk, wvt):
    b, s, d = x.shape
    rows = QKV_ROWS
    const = lambda bi, i: (0, 0)
    return pl.pallas_call(
        _qkv_kernel,
        grid=(b, s // rows),
        in_specs=[
            pl.BlockSpec((1, rows, d), lambda bi, i: (bi, i, 0)),
            pl.BlockSpec((1, d), const),
            pl.BlockSpec((d, d), const),
            pl.BlockSpec((d, d), const),
            pl.BlockSpec((d, d), const),
        ],
        out_specs=[
            pl.BlockSpec((1, d, rows), lambda bi, i: (bi, 0, i)),
            pl.BlockSpec((1, rows, d), lambda bi, i: (bi, i, 0)),
            pl.BlockSpec((1, ATTN_HEADS * V_AUG, rows), lambda bi, i: (bi, 0, i)),
        ],
        out_shape=[
            jax.ShapeDtypeStruct((b, d, s), BF16),
            jax.ShapeDtypeStruct((b, s, d), BF16),
            jax.ShapeDtypeStruct((b, ATTN_HEADS * V_AUG, s), BF16),
        ],
        compiler_params=pltpu.CompilerParams(
            dimension_semantics=("parallel", "parallel"),
            vmem_limit_bytes=VMEM_LIMIT_BYTES),
        name="qkv_proj",
    )(x, gain, wqt, wk, wvt)


def _attn_kernel(shift_ref, qt_ref, k_ref, vt_ref, bias_ref, lq1_ref, lk1_ref, lq2_ref,
                 lk2_ref, subln_ref, o_ref, w_ref, m_ref, acc_ref,
                 alpha_a, alpha_b, s_a, s_b, p_a, p_b):
    tq, tk = ATTN_TQ, ATTN_TK
    buf_a = (s_a, p_a, alpha_a)
    buf_b = (s_b, p_b, alpha_b)
    head = pl.program_id(1)
    qi = pl.program_id(2)
    shift = shift_ref[head]

    qt = qt_ref[0]
    zeros = jnp.zeros((HEAD_DIM, tq), BF16)
    row = lax.broadcasted_iota(jnp.int32, (V_DIM, tq), 0)
    ones = jnp.where(row < 2, 1.0, 0.0).astype(BF16)
    w_ref[:, 0:tq] = jnp.concatenate([qt[0:HEAD_DIM], zeros, ones], axis=0)
    w_ref[:, tq:2 * tq] = jnp.concatenate([zeros, qt[HEAD_DIM:V_DIM], ones], axis=0)
    m_ref[...] = jnp.full(m_ref.shape, MASK_VALUE, F32)
    acc_ref[...] = jnp.zeros(acc_ref.shape, F32)
    for _, p_buf, alpha_buf in (buf_a, buf_b):
        alpha_buf[...] = jnp.ones(alpha_buf.shape, F32)
        p_buf[...] = jnp.zeros(p_buf.shape, BF16)

    def scores(j, buf):
        start = pl.multiple_of(j * tk, tk)
        k_aug = jnp.concatenate([k_ref[0, pl.ds(start, tk), :], bias_ref[0]], axis=1)
        buf[0][...] = jnp.dot(k_aug, w_ref[...], preferred_element_type=F32)

    def softmax(buf, causal):
        s_buf, p_buf, alpha_buf = buf
        for mi in range(2):
            sm = s_buf[:, mi * tq:(mi + 1) * tq]
            if causal:
                key_pos = lax.broadcasted_iota(jnp.int32, (tk, tq), 0)
                qry_pos = lax.broadcasted_iota(jnp.int32, (tk, tq), 1)
                sm = jnp.where(key_pos <= qry_pos, sm, MASK_VALUE)
            m_prev = m_ref[mi] - shift
            m_new = jnp.maximum(m_prev, jnp.max(sm, axis=0, keepdims=True))
            alpha_buf[mi] = jnp.exp(m_prev - m_new)
            m_ref[mi] = m_new
            p_buf[mi] = jnp.exp2(((sm - m_new) * LOG2E).astype(BF16))

    def pv_update(j, buf):
        _, p_buf, alpha_buf = buf
        start = pl.multiple_of(j * tk, tk)
        vt = vt_ref[0, :, pl.ds(start, tk)]
        for mi in range(2):
            acc_ref[mi] = alpha_buf[mi] * acc_ref[mi] + jnp.dot(
                vt, p_buf[mi], preferred_element_type=F32)

    def step(j, cur, nxt):
        scores(j + 1, nxt)
        pv_update(jnp.maximum(j - 2, 0), cur)
        softmax(cur, False)

    def finish(cur, nxt):
        pv_update(jnp.maximum(qi - 2, 0), cur)
        softmax(cur, True)
        pv_update(jnp.maximum(qi - 1, 0), nxt)
        pv_update(qi, cur)

    scores(0, buf_a)

    def two_steps(t, carry):
        step(2 * t, buf_a, buf_b)
        step(2 * t + 1, buf_b, buf_a)
        return carry

    lax.fori_loop(0, qi // 2, two_steps, 0)

    @pl.when(qi % 2 == 1)
    def _():
        step(qi - 1, buf_a, buf_b)
        finish(buf_b, buf_a)

    @pl.when(qi % 2 == 0)
    def _():
        finish(buf_a, buf_b)

    lam = (jnp.exp(jnp.sum(lq1_ref[...] * lk1_ref[...], keepdims=True))
           - jnp.exp(jnp.sum(lq2_ref[...] * lk2_ref[...], keepdims=True)) + LAM_INIT)
    o1 = acc_ref[0, 0:V_DIM, :] * (1.0 / acc_ref[0, V_DIM:V_DIM + 1, :])
    o2 = acc_ref[1, 0:V_DIM, :] * (1.0 / acc_ref[1, V_DIM:V_DIM + 1, :])
    ot = o1 - lam * o2
    o = ot.T
    o_ref[0] = (_rms(o, subln_ref[...]) * (1.0 - LAM_INIT)).astype(BF16)


def _alibi_tables():
    slopes = [2.0 ** (-8.0 * (h + 1) / ATTN_HEADS) for h in range(ATTN_HEADS)]
    pos = jnp.arange(ATTN_TK)
    low = (pos % MXU_DIM).astype(F32)
    high = (pos - pos % MXU_DIM).astype(F32)
    tile = jnp.zeros((ATTN_HEADS, ATTN_TK, V_DIM), F32)
    sl = jnp.asarray(slopes, F32)[:, None]
    tile = tile.at[:, :, 0].set(sl * low[None, :])
    tile = tile.at[:, :, 1].set(sl * high[None, :])
    shifts = jnp.asarray([s * ATTN_TK for s in slopes], F32)
    return tile.astype(BF16), shifts


def _attn_call(qt, k, vt, lq1, lk1, lq2, lk2, subln):
    b, s, d = k.shape
    tq, tk = ATTN_TQ, ATTN_TK
    bias, shifts = _alibi_tables()
    vec = lambda n: pl.BlockSpec((1, n), lambda bi, h, i: (0, 0))
    return pl.pallas_call(
        _attn_kernel,
        grid=(b, ATTN_HEADS, s // tq),
        in_specs=[
            pl.BlockSpec(memory_space=pltpu.SMEM),
            pl.BlockSpec((1, V_DIM, tq), lambda bi, h, i: (bi, h, i)),
            pl.BlockSpec((1, s, V_DIM), lambda bi, h, i: (bi, 0, h)),
            pl.BlockSpec((1, V_AUG, s), lambda bi, h, i: (bi, h, 0)),
            pl.BlockSpec((1, tk, V_DIM), lambda bi, h, i: (h, 0, 0)),
            vec(HEAD_DIM), vec(HEAD_DIM), vec(HEAD_DIM), vec(HEAD_DIM),
            vec(V_DIM),
        ],
        out_specs=pl.BlockSpec((1, tq, V_DIM), lambda bi, h, i: (bi, i, h)),
        out_shape=jax.ShapeDtypeStruct((b, s, d), BF16),
        scratch_shapes=[
            pltpu.VMEM((2 * V_DIM, 2 * tq), BF16),
            pltpu.VMEM((2, 1, tq), F32),
            pltpu.VMEM((2, V_AUG, tq), F32),
            pltpu.VMEM((2, 1, tq), F32),
            pltpu.VMEM((2, 1, tq), F32),
            pltpu.VMEM((tk, 2 * tq), F32),
            pltpu.VMEM((tk, 2 * tq), F32),
            pltpu.VMEM((2, tk, tq), BF16),
            pltpu.VMEM((2, tk, tq), BF16),
        ],
        compiler_params=pltpu.CompilerParams(
            dimension_semantics=("parallel", "parallel", "arbitrary"),
            vmem_limit_bytes=VMEM_LIMIT_BYTES),
        name="diff_attn",
    )(shifts, qt, k, vt, bias, lq1, lk1, lq2, lk2, subln)


def _oproj_kernel(x_ref, a_ref, w_ref, g_ref, o_ref):
    m = jnp.dot(a_ref[...], w_ref[...], preferred_element_type=F32)
    o_ref[...] = x_ref[...] + _rms(m, g_ref[...])


def _oproj_call(x2d, a2d, w, gain):
    t, d = x2d.shape
    rows = ROW_TILE
    return pl.pallas_call(
        _oproj_kernel,
        grid=(t // rows,),
        in_specs=[
            pl.BlockSpec((rows, d), lambda i: (i, 0)),
            pl.BlockSpec((rows, d), lambda i: (i, 0)),
            pl.BlockSpec((d, d), lambda i: (0, 0)),
            pl.BlockSpec((1, d), lambda i: (0, 0)),
        ],
        out_specs=pl.BlockSpec((rows, d), lambda i: (i, 0)),
        out_shape=jax.ShapeDtypeStruct((t, d), F32),
        compiler_params=pltpu.CompilerParams(
            dimension_semantics=("parallel",),
            vmem_limit_bytes=VMEM_LIMIT_BYTES),
        name="attn_oproj",
    )(x2d, a2d, w, gain)


def _ffn_kernel(x_ref, gpre_ref, wa_ref, wg_ref, cw_ref, wd_ref, gpost_ref, o_ref,
                hn_ref, abuf_ref, carry_ref, act_ref, *, tiles_per_seq):
    rows = x_ref.shape[0]
    seq_start = (pl.program_id(0) % tiles_per_seq) == 0
    x = x_ref[...]
    hn_ref[...] = _rms(x, gpre_ref[...]).astype(BF16)
    for c in range(FFN_DIM // FFN_CHUNK):
        cols = slice(c * FFN_CHUNK, (c + 1) * FFN_CHUNK)
        a = jnp.dot(hn_ref[...], wa_ref[:, cols], preferred_element_type=F32)
        abuf_ref[0:SUBLANES, :] = jnp.where(seq_start, 0.0, carry_ref[c])
        abuf_ref[SUBLANES:SUBLANES + rows, :] = a
        carry_ref[c] = a[rows - SUBLANES:rows, :]
        cw = cw_ref[:, cols]
        a_conv = (cw[3:4] + abuf_ref[SUBLANES - 2:SUBLANES - 2 + rows, :] * cw[0:1]
                  + abuf_ref[SUBLANES - 1:SUBLANES - 1 + rows, :] * cw[1:2] + a * cw[2:3])
        g = jnp.dot(hn_ref[...], wg_ref[:, cols], preferred_element_type=F32)
        act_ref[:, cols] = (_gelu_tanh(a_conv) * g).astype(BF16)
    f = jnp.dot(act_ref[...], wd_ref[...], preferred_element_type=F32)
    o_ref[...] = x + _rms(f, gpost_ref[...])


def _ffn_call(x2d, seq_len, gpre, w_up, conv_tab, w_down, gpost):
    t, d = x2d.shape
    rows = ROW_TILE
    const = lambda i: (0, 0)
    resident = pl.Buffered(1)
    return pl.pallas_call(
        functools.partial(_ffn_kernel, tiles_per_seq=seq_len // rows),
        grid=(t // rows,),
        in_specs=[
            pl.BlockSpec((rows, d), lambda i: (i, 0)),
            pl.BlockSpec((1, d), const),
            pl.BlockSpec((d, FFN_DIM), lambda i: (0, 0), pipeline_mode=resident),
            pl.BlockSpec((d, FFN_DIM), lambda i: (0, 1), pipeline_mode=resident),
            pl.BlockSpec((SUBLANES, FFN_DIM), const),
            pl.BlockSpec((FFN_DIM, d), const, pipeline_mode=resident),
            pl.BlockSpec((1, d), const),
        ],
        out_specs=pl.BlockSpec((rows, d), lambda i: (i, 0)),
        out_shape=jax.ShapeDtypeStruct((t, d), F32),
        scratch_shapes=[
            pltpu.VMEM((rows, d), BF16),
            pltpu.VMEM((rows + SUBLANES, FFN_CHUNK), F32),
            pltpu.VMEM((FFN_DIM // FFN_CHUNK, SUBLANES, FFN_CHUNK), F32),
            pltpu.VMEM((rows, FFN_DIM), BF16),
        ],
        compiler_params=pltpu.CompilerParams(
            dimension_semantics=("arbitrary",),
            vmem_limit_bytes=VMEM_LIMIT_BYTES),
        name="conv_ffn",
    )(x2d, gpre, w_up, w_up, conv_tab, w_down, gpost)


def _sgu_kernel(x_ref, gpre_ref, win_ref, lng_ref, lnb_ref, ws_ref, bst_ref, wout_ref,
                gpost_ref, o_ref, hn_ref, v_ref, vn_ref, y_ref):
    rows = x_ref.shape[0]
    gd = SGU_GROUP_DIM
    x = x_ref[...]
    hn_ref[...] = _rms(x, gpre_ref[...]).astype(BF16)
    for g in range(SGU_GROUPS):
        cols = slice(SGU_WIDTH + g * gd, SGU_WIDTH + (g + 1) * gd)
        v_ref[:, g * gd:(g + 1) * gd] = _gelu_tanh(
            jnp.dot(hn_ref[...], win_ref[:, cols], preferred_element_type=F32))
    v = v_ref[...]
    mu = jnp.mean(v, axis=-1, keepdims=True)
    vc = v - mu
    var = jnp.mean(vc * vc, axis=-1, keepdims=True)
    vn_ref[...] = (vc * lax.rsqrt(var + LN_EPS) * lng_ref[...] + lnb_ref[...]).astype(BF16)
    t_pos = lax.broadcasted_iota(jnp.int32, (CHUNK, CHUNK), 0)
    s_pos = lax.broadcasted_iota(jnp.int32, (CHUNK, CHUNK), 1)
    for g in range(SGU_GROUPS):
        cols = slice(g * gd, (g + 1) * gd)
        u = _gelu_tanh(jnp.dot(hn_ref[...], win_ref[:, cols], preferred_element_type=F32))
        w = jnp.where(s_pos <= t_pos, ws_ref[g], 0.0).astype(BF16)
        bias = bst_ref[:, g:g + 1]
        for c in range(rows // CHUNK):
            r = slice(c * CHUNK, (c + 1) * CHUNK)
            s = jnp.dot(w, vn_ref[r, cols], preferred_element_type=F32) + bias
            y_ref[r, cols] = (u[r] * s).astype(BF16)
    out = jnp.dot(y_ref[...], wout_ref[...], preferred_element_type=F32)
    o_ref[...] = x + _rms(out, gpost_ref[...])


def _sgu_call(x2d, gpre, w_in, ln_g, ln_b, w_s, b_st, w_out, gpost):
    t, d = x2d.shape
    rows = ROW_TILE
    const = lambda i: (0, 0)
    resident = pl.Buffered(1)
    return pl.pallas_call(
        _sgu_kernel,
        grid=(t // rows,),
        in_specs=[
            pl.BlockSpec((rows, d), lambda i: (i, 0)),
            pl.BlockSpec((1, d), const),
            pl.BlockSpec((d, 2 * SGU_WIDTH), const, pipeline_mode=resident),
            pl.BlockSpec((1, SGU_WIDTH), const),
            pl.BlockSpec((1, SGU_WIDTH), const),
            pl.BlockSpec((SGU_GROUPS, CHUNK, CHUNK), lambda i: (0, 0, 0)),
            pl.BlockSpec((CHUNK, SGU_GROUPS), const),
            pl.BlockSpec((SGU_WIDTH, d), const, pipeline_mode=resident),
            pl.BlockSpec((1, d), const),
        ],
        out_specs=pl.BlockSpec((rows, d), lambda i: (i, 0)),
        out_shape=jax.ShapeDtypeStruct((t, d), F32),
        scratch_shapes=[
            pltpu.VMEM((rows, d), BF16),
            pltpu.VMEM((rows, SGU_WIDTH), F32),
            pltpu.VMEM((rows, SGU_WIDTH), BF16),
            pltpu.VMEM((rows, SGU_WIDTH), BF16),
        ],
        compiler_params=pltpu.CompilerParams(
            dimension_semantics=("parallel",),
            vmem_limit_bytes=VMEM_LIMIT_BYTES),
        name="chunked_sgu",
    )(x2d, gpre, w_in, ln_g, ln_b, w_s, b_st, w_out, gpost)


def _conv_table(conv_w, conv_b):
    pad = jnp.zeros((SUBLANES - 4, FFN_DIM), F32)
    return jnp.concatenate([conv_w, conv_b[None, :], pad], axis=0)


def _row(v):
    return v.reshape(1, -1)


def kernel(x, attn_w_qkv, attn_lambda_q1, attn_lambda_k1, attn_lambda_q2, attn_lambda_k2, attn_subln, attn_w_o, sgu_w_in, sgu_ln_g, sgu_ln_b, sgu_w_s, sgu_b_s, sgu_w_out, norm_mix_pre, norm_mix_post, norm_ffn_pre, norm_ffn_post, ffn_w_up, ffn_conv_w, ffn_conv_b, ffn_w_down):
    b, s, d = x.shape
    assert d == D_MODEL and s % max(QKV_ROWS, ROW_TILE, ATTN_TQ) == 0 and ATTN_TQ == ATTN_TK

    w_qkv = attn_w_qkv[0]
    wqt = (w_qkv[:, :d].T * (HEAD_DIM ** -0.5)).astype(BF16)
    wk = w_qkv[:, d:2 * d].astype(BF16)
    wvt = w_qkv[:, 2 * d:].T.astype(BF16)
    qt, k, vt = _qkv_call(x, _row(norm_mix_pre[0]), wqt, wk, wvt)
    attn = _attn_call(qt, k, vt, _row(attn_lambda_q1[0]), _row(attn_lambda_k1[0]),
                      _row(attn_lambda_q2[0]), _row(attn_lambda_k2[0]), _row(attn_subln[0]))
    x2d = x.reshape(b * s, d)
    x2d = _oproj_call(x2d, attn.reshape(b * s, d), attn_w_o[0].astype(BF16),
                      _row(norm_mix_post[0]))
    x2d = _ffn_call(x2d, s, _row(norm_ffn_pre[0]), ffn_w_up[0].astype(BF16),
                    _conv_table(ffn_conv_w[0], ffn_conv_b[0]), ffn_w_down[0].astype(BF16),
                    _row(norm_ffn_post[0]))

    x2d = _sgu_call(x2d, _row(norm_mix_pre[1]), sgu_w_in[0].astype(BF16), _row(sgu_ln_g[0]),
                    _row(sgu_ln_b[0]), sgu_w_s[0], sgu_b_s[0].T, sgu_w_out[0].astype(BF16),
                    _row(norm_mix_post[1]))
    x2d = _ffn_call(x2d, s, _row(norm_ffn_pre[1]), ffn_w_up[1].astype(BF16),
                    _conv_table(ffn_conv_w[1], ffn_conv_b[1]), ffn_w_down[1].astype(BF16),
                    _row(norm_ffn_post[1]))
    return x2d.reshape(b, s, d)
```

```python
import functools
import math

import jax
import jax.numpy as jnp
from jax import lax
from jax.experimental import pallas as pl
from jax.experimental.pallas import tpu as pltpu

F32 = jnp.float32
BF16 = jnp.bfloat16

D_MODEL = 1024
ATTN_HEADS = 8
HEAD_DIM = 64
V_DIM = 2 * HEAD_DIM
SGU_WIDTH = 2 * D_MODEL
SGU_GROUPS = 8
SGU_GROUP_DIM = SGU_WIDTH // SGU_GROUPS
CHUNK = 128
FFN_DIM = 2816
NORM_EPS = 1e-6
LN_EPS = 1e-5
LAM_INIT = 0.8 - 0.6 * math.exp(-0.3 * 0)

MXU_DIM = 256
SUBLANES = 8
V_PAD = 16
V_AUG = V_DIM + V_PAD
LOG2E = math.log2(math.e)
VMEM_LIMIT_BYTES = 56 * 1024 * 1024

QKV_ROWS = 512
ATTN_TQ = 256
ATTN_TK = 256
ROW_TILE = 512
FFN_CHUNK = MXU_DIM
MASK_VALUE = -1e30
KNORM_ROWS = 1024
SKIP_MARGIN = 110.0
NORM_SLACK = 1.02

_NT_DIMS = (((1,), (1,)), ((), ()))


def _rms(x, gain):
    return x * lax.rsqrt(jnp.mean(x * x, axis=-1, keepdims=True) + NORM_EPS) * gain


def _gelu_tanh(x):
    return x * (0.5 * (1.0 + jnp.tanh(0.7978845608028654 * (x + 0.044715 * (x * x * x)))))


def _qkv_kernel(x_ref, g_ref, wqt_ref, wk_ref, wvt_ref, qt_ref, k_ref, vt_ref):
    rows = x_ref.shape[1]
    hn = _rms(x_ref[0], g_ref[...]).astype(BF16)
    k_ref[0] = jnp.dot(hn, wk_ref[...], preferred_element_type=F32).astype(BF16)
    qt_ref[0] = lax.dot_general(wqt_ref[...], hn, _NT_DIMS,
                                preferred_element_type=F32).astype(BF16)
    vt = lax.dot_general(wvt_ref[...], hn, _NT_DIMS,
                         preferred_element_type=F32).astype(BF16)
    pad_row = lax.broadcasted_iota(jnp.int32, (V_PAD, rows), 0)
    ones_rows = jnp.where(pad_row == 0, 1.0, 0.0).astype(BF16)
    for h in range(ATTN_HEADS):
        vt_ref[0, h * V_AUG:h * V_AUG + V_DIM, :] = vt[h * V_DIM:(h + 1) * V_DIM]
        vt_ref[0, h * V_AUG + V_DIM:(h + 1) * V_AUG, :] = ones_rows


def _qkv_call(x, gain, wqt, wk, wvt):
    b, s, d = x.shape
    rows = QKV_ROWS
    const = lambda bi, i: (0, 0)
    return pl.pallas_call(
        _qkv_kernel,
        grid=(b, s // rows),
        in_specs=[
            pl.BlockSpec((1, rows, d), lambda bi, i: (bi, i, 0)),
            pl.BlockSpec((1, d), const),
            pl.BlockSpec((d, d), const),
            pl.BlockSpec((d, d), const),
            pl.BlockSpec((d, d), const),
        ],
        out_specs=[
            pl.BlockSpec((1, d, rows), lambda bi, i: (bi, 0, i)),
            pl.BlockSpec((1, rows, d), lambda bi, i: (bi, i, 0)),
            pl.BlockSpec((1, ATTN_HEADS * V_AUG, rows), lambda bi, i: (bi, 0, i)),
        ],
        out_shape=[
            jax.ShapeDtypeStruct((b, d, s), BF16),
            jax.ShapeDtypeStruct((b, s, d), BF16),
            jax.ShapeDtypeStruct((b, ATTN_HEADS * V_AUG, s), BF16),
        ],
        compiler_params=pltpu.CompilerParams(
            dimension_semantics=("parallel", "parallel"),
            vmem_limit_bytes=VMEM_LIMIT_BYTES),
        name="qkv_proj",
    )(x, gain, wqt, wk, wvt)


def _attn_kernel(shift_ref, qt_ref, k_ref, vt_ref, bias_ref, lq1_ref, lk1_ref, lq2_ref,
                 lk2_ref, subln_ref, o_ref, w_ref, m_ref, acc_ref,
                 alpha_a, alpha_b, s_a, s_b, p_a, p_b, kmax_ref):
    tq, tk = ATTN_TQ, ATTN_TK
    buf_a = (s_a, p_a, alpha_a)
    buf_b = (s_b, p_b, alpha_b)
    head = pl.program_id(1)
    qi = pl.program_id(2)
    shift = shift_ref[0, head]
    inv_slope = shift_ref[1, head]

    @pl.when(qi == 0)
    def _():
        dim = lax.broadcasted_iota(jnp.int32, (V_DIM, V_DIM), 0)
        col = lax.broadcasted_iota(jnp.int32, (V_DIM, V_DIM), 1)
        map_of_dim = jnp.where(dim >= HEAD_DIM, 1, 0)
        select = jnp.where(col == map_of_dim, 1.0, 0.0).astype(BF16)

        def chunk_max(c, best):
            kk = k_ref[0, pl.ds(pl.multiple_of(c * KNORM_ROWS, KNORM_ROWS), KNORM_ROWS), :]
            norms = jnp.dot(kk * kk, select, preferred_element_type=F32)
            return jnp.maximum(best, jnp.max(norms, axis=0, keepdims=True))

        kmax_ref[...] = lax.fori_loop(0, k_ref.shape[1] // KNORM_ROWS, chunk_max,
                                      jnp.zeros((1, V_DIM), F32))

    qt = qt_ref[0]

    qf = qt.astype(F32)
    ktf = k_ref[0, pl.ds(pl.multiple_of(qi * tq, tq), tq), :].astype(F32).T
    bound = None
    for mi in range(2):
        rows = slice(mi * HEAD_DIM, (mi + 1) * HEAD_DIM)
        q_sq = jnp.max(jnp.sum(qf[rows] * qf[rows], axis=0, keepdims=True), axis=1, keepdims=True)
        self_min = jnp.min(jnp.sum(qf[rows] * ktf[rows], axis=0, keepdims=True), axis=1,
                           keepdims=True)
        k_sq = kmax_ref[:, mi:mi + 1]
        w_map = NORM_SLACK * jnp.sqrt(q_sq * k_sq) - self_min + SKIP_MARGIN
        bound = w_map if bound is None else jnp.maximum(bound, w_map)
    first_query = jnp.full((1, 1), qi * tq, jnp.int32).astype(F32)
    skip_f = jnp.floor((first_query - bound * inv_slope) * (1.0 / tk))
    j0 = jnp.clip(jnp.clip(skip_f, 0.0, float(2 ** 20)).astype(jnp.int32)[0, 0], 0, qi)

    zeros = jnp.zeros((HEAD_DIM, tq), BF16)
    row = lax.broadcasted_iota(jnp.int32, (V_DIM, tq), 0)
    ones = jnp.where(row < 2, 1.0, 0.0).astype(BF16)
    w_ref[:, 0:tq] = jnp.concatenate([qt[0:HEAD_DIM], zeros, ones], axis=0)
    w_ref[:, tq:2 * tq] = jnp.concatenate([zeros, qt[HEAD_DIM:V_DIM], ones], axis=0)
    m_ref[...] = jnp.full(m_ref.shape, MASK_VALUE, F32)
    acc_ref[...] = jnp.zeros(acc_ref.shape, F32)
    for _, p_buf, alpha_buf in (buf_a, buf_b):
        alpha_buf[...] = jnp.ones(alpha_buf.shape, F32)
        p_buf[...] = jnp.zeros(p_buf.shape, BF16)

    def scores(j, buf):
        start = pl.multiple_of(j * tk, tk)
        k_aug = jnp.concatenate([k_ref[0, pl.ds(start, tk), :], bias_ref[0]], axis=1)
        buf[0][...] = jnp.dot(k_aug, w_ref[...], preferred_element_type=F32)

    def softmax(buf, causal):
        s_buf, p_buf, alpha_buf = buf
        for mi in range(2):
            sm = s_buf[:, mi * tq:(mi + 1) * tq]
            if causal:
                key_pos = lax.broadcasted_iota(jnp.int32, (tk, tq), 0)
                qry_pos = lax.broadcasted_iota(jnp.int32, (tk, tq), 1)
                sm = jnp.where(key_pos <= qry_pos, sm, MASK_VALUE)
            m_prev = m_ref[mi] - shift
            m_new = jnp.maximum(m_prev, jnp.max(sm, axis=0, keepdims=True))
            alpha_buf[mi] = jnp.exp(m_prev - m_new)
            m_ref[mi] = m_new
            p_buf[mi] = jnp.exp2(((sm - m_new) * LOG2E).astype(BF16))

    def pv_update(j, buf):
        _, p_buf, alpha_buf = buf
        start = pl.multiple_of(j * tk, tk)
        vt = vt_ref[0, :, pl.ds(start, tk)]
        for mi in range(2):
            acc_ref[mi] = alpha_buf[mi] * acc_ref[mi] + jnp.dot(
                vt, p_buf[mi], preferred_element_type=F32)

    def step(j, cur, nxt):
        scores(j + 1, nxt)
        pv_update(jnp.maximum(j - 2, 0), cur)
        softmax(cur, False)

    def finish(cur, nxt):
        pv_update(jnp.maximum(qi - 2, 0), cur)
        softmax(cur, True)
        pv_update(jnp.maximum(qi - 1, 0), nxt)
        pv_update(qi, cur)

    scores(j0, buf_a)
    n_full = qi - j0

    def two_steps(t, carry):
        step(j0 + 2 * t, buf_a, buf_b)
        step(j0 + 2 * t + 1, buf_b, buf_a)
        return carry

    lax.fori_loop(0, n_full // 2, two_steps, 0)

    @pl.when(n_full % 2 == 1)
    def _():
        step(qi - 1, buf_a, buf_b)
        finish(buf_b, buf_a)

    @pl.when(n_full % 2 == 0)
    def _():
        finish(buf_a, buf_b)

    lam = (jnp.exp(jnp.sum(lq1_ref[...] * lk1_ref[...], keepdims=True))
           - jnp.exp(jnp.sum(lq2_ref[...] * lk2_ref[...], keepdims=True)) + LAM_INIT)
    o1 = acc_ref[0, 0:V_DIM, :] * (1.0 / acc_ref[0, V_DIM:V_DIM + 1, :])
    o2 = acc_ref[1, 0:V_DIM, :] * (1.0 / acc_ref[1, V_DIM:V_DIM + 1, :])
    ot = o1 - lam * o2
    o = ot.T
    o_ref[0] = (_rms(o, subln_ref[...]) * (1.0 - LAM_INIT)).astype(BF16)


def _alibi_tables():
    slopes = [2.0 ** (-8.0 * (h + 1) / ATTN_HEADS) for h in range(ATTN_HEADS)]
    pos = jnp.arange(ATTN_TK)
    low = (pos % MXU_DIM).astype(F32)
    high = (pos - pos % MXU_DIM).astype(F32)
    tile = jnp.zeros((ATTN_HEADS, ATTN_TK, V_DIM), F32)
    sl = jnp.asarray(slopes, F32)[:, None]
    tile = tile.at[:, :, 0].set(sl * low[None, :])
    tile = tile.at[:, :, 1].set(sl * high[None, :])
    scalars = jnp.asarray([[s * ATTN_TK for s in slopes], [1.0 / s for s in slopes]], F32)
    return tile.astype(BF16), scalars


def _attn_call(qt, k, vt, lq1, lk1, lq2, lk2, subln):
    b, s, d = k.shape
    tq, tk = ATTN_TQ, ATTN_TK
    bias, shifts = _alibi_tables()
    vec = lambda n: pl.BlockSpec((1, n), lambda bi, h, i: (0, 0))
    return pl.pallas_call(
        _attn_kernel,
        grid=(b, ATTN_HEADS, s // tq),
        in_specs=[
            pl.BlockSpec(memory_space=pltpu.SMEM),
            pl.BlockSpec((1, V_DIM, tq), lambda bi, h, i: (bi, h, i)),
            pl.BlockSpec((1, s, V_DIM), lambda bi, h, i: (bi, 0, h)),
            pl.BlockSpec((1, V_AUG, s), lambda bi, h, i: (bi, h, 0)),
            pl.BlockSpec((1, tk, V_DIM), lambda bi, h, i: (h, 0, 0)),
            vec(HEAD_DIM), vec(HEAD_DIM), vec(HEAD_DIM), vec(HEAD_DIM),
            vec(V_DIM),
        ],
        out_specs=pl.BlockSpec((1, tq, V_DIM), lambda bi, h, i: (bi, i, h)),
        out_shape=jax.ShapeDtypeStruct((b, s, d), BF16),
        scratch_shapes=[
            pltpu.VMEM((2 * V_DIM, 2 * tq), BF16),
            pltpu.VMEM((2, 1, tq), F32),
            pltpu.VMEM((2, V_AUG, tq), F32),
            pltpu.VMEM((2, 1, tq), F32),
            pltpu.VMEM((2, 1, tq), F32),
            pltpu.VMEM((tk, 2 * tq), F32),
            pltpu.VMEM((tk, 2 * tq), F32),
            pltpu.VMEM((2, tk, tq), BF16),
            pltpu.VMEM((2, tk, tq), BF16),
            pltpu.VMEM((1, V_DIM), F32),
        ],
        compiler_params=pltpu.CompilerParams(
            dimension_semantics=("parallel", "parallel", "arbitrary"),
            vmem_limit_bytes=VMEM_LIMIT_BYTES),
        name="diff_attn",
    )(shifts, qt, k, vt, bias, lq1, lk1, lq2, lk2, subln)


def _oproj_kernel(x_ref, a_ref, w_ref, g_ref, o_ref):
    m = jnp.dot(a_ref[...], w_ref[...], preferred_element_type=F32)
    o_ref[...] = x_ref[...] + _rms(m, g_ref[...])


def _oproj_call(x2d, a2d, w, gain):
    t, d = x2d.shape
    rows = ROW_TILE
    return pl.pallas_call(
        _oproj_kernel,
        grid=(t // rows,),
        in_specs=[
            pl.BlockSpec((rows, d), lambda i: (i, 0)),
            pl.BlockSpec((rows, d), lambda i: (i, 0)),
            pl.BlockSpec((d, d), lambda i: (0, 0)),
            pl.BlockSpec((1, d), lambda i: (0, 0)),
        ],
        out_specs=pl.BlockSpec((rows, d), lambda i: (i, 0)),
        out_shape=jax.ShapeDtypeStruct((t, d), F32),
        compiler_params=pltpu.CompilerParams(
            dimension_semantics=("parallel",),
            vmem_limit_bytes=VMEM_LIMIT_BYTES),
        name="attn_oproj",
    )(x2d, a2d, w, gain)


def _ffn_kernel(x_ref, gpre_ref, wa_ref, wg_ref, cw_ref, wd_ref, gpost_ref, o_ref,
                hn_ref, abuf_ref, carry_ref, act_ref, *, tiles_per_seq):
    rows = x_ref.shape[0]
    seq_start = (pl.program_id(0) % tiles_per_seq) == 0
    x = x_ref[...]
    hn_ref[...] = _rms(x, gpre_ref[...]).astype(BF16)
    for c in range(FFN_DIM // FFN_CHUNK):
        cols = slice(c * FFN_CHUNK, (c + 1) * FFN_CHUNK)
        a = jnp.dot(hn_ref[...], wa_ref[:, cols], preferred_element_type=F32)
        abuf_ref[0:SUBLANES, :] = jnp.where(seq_start, 0.0, carry_ref[c])
        abuf_ref[SUBLANES:SUBLANES + rows, :] = a
        carry_ref[c] = a[rows - SUBLANES:rows, :]
        cw = cw_ref[:, cols]
        a_conv = (cw[3:4] + abuf_ref[SUBLANES - 2:SUBLANES - 2 + rows, :] * cw[0:1]
                  + abuf_ref[SUBLANES - 1:SUBLANES - 1 + rows, :] * cw[1:2] + a * cw[2:3])
        g = jnp.dot(hn_ref[...], wg_ref[:, cols], preferred_element_type=F32)
        act_ref[:, cols] = (_gelu_tanh(a_conv) * g).astype(BF16)
    f = jnp.dot(act_ref[...], wd_ref[...], preferred_element_type=F32)
    o_ref[...] = x + _rms(f, gpost_ref[...])


def _ffn_call(x2d, seq_len, gpre, w_up, conv_tab, w_down, gpost):
    t, d = x2d.shape
    rows = ROW_TILE
    const = lambda i: (0, 0)
    resident = pl.Buffered(1)
    return pl.pallas_call(
        functools.partial(_ffn_kernel, tiles_per_seq=seq_len // rows),
        grid=(t // rows,),
        in_specs=[
            pl.BlockSpec((rows, d), lambda i: (i, 0)),
            pl.BlockSpec((1, d), const),
            pl.BlockSpec((d, FFN_DIM), lambda i: (0, 0), pipeline_mode=resident),
            pl.BlockSpec((d, FFN_DIM), lambda i: (0, 1), pipeline_mode=resident),
            pl.BlockSpec((SUBLANES, FFN_DIM), const),
            pl.BlockSpec((FFN_DIM, d), const, pipeline_mode=resident),
            pl.BlockSpec((1, d), const),
        ],
        out_specs=pl.BlockSpec((rows, d), lambda i: (i, 0)),
        out_shape=jax.ShapeDtypeStruct((t, d), F32),
        scratch_shapes=[
            pltpu.VMEM((rows, d), BF16),
            pltpu.VMEM((rows + SUBLANES, FFN_CHUNK), F32),
            pltpu.VMEM((FFN_DIM // FFN_CHUNK, SUBLANES, FFN_CHUNK), F32),
            pltpu.VMEM((rows, FFN_DIM), BF16),
        ],
        compiler_params=pltpu.CompilerParams(
            dimension_semantics=("arbitrary",),
            vmem_limit_bytes=VMEM_LIMIT_BYTES),
        name="conv_ffn",
    )(x2d, gpre, w_up, w_up, conv_tab, w_down, gpost)


def _sgu_kernel(x_ref, gpre_ref, win_ref, lng_ref, lnb_ref, ws_ref, bst_ref, wout_ref,
                gpost_ref, o_ref, hn_ref, v_ref, vn_ref, y_ref):
    rows = x_ref.shape[0]
    gd = SGU_GROUP_DIM
    x = x_ref[...]
    hn_ref[...] = _rms(x, gpre_ref[...]).astype(BF16)
    for g in range(SGU_GROUPS):
        cols = slice(SGU_WIDTH + g * gd, SGU_WIDTH + (g + 1) * gd)
        v_ref[:, g * gd:(g + 1) * gd] = _gelu_tanh(
            jnp.dot(hn_ref[...], win_ref[:, cols], preferred_element_type=F32))
    v = v_ref[...]
    mu = jnp.mean(v, axis=-1, keepdims=True)
    vc = v - mu
    var = jnp.mean(vc * vc, axis=-1, keepdims=True)
    vn_ref[...] = (vc * lax.rsqrt(var + LN_EPS) * lng_ref[...] + lnb_ref[...]).astype(BF16)
    t_pos = lax.broadcasted_iota(jnp.int32, (CHUNK, CHUNK), 0)
    s_pos = lax.broadcasted_iota(jnp.int32, (CHUNK, CHUNK), 1)
    for g in range(SGU_GROUPS):
        cols = slice(g * gd, (g + 1) * gd)
        u = _gelu_tanh(jnp.dot(hn_ref[...], win_ref[:, cols], preferred_element_type=F32))
        w = jnp.where(s_pos <= t_pos, ws_ref[g], 0.0).astype(BF16)
        bias = bst_ref[:, g:g + 1]
        for c in range(rows // CHUNK):
            r = slice(c * CHUNK, (c + 1) * CHUNK)
            s = jnp.dot(w, vn_ref[r, cols], preferred_element_type=F32) + bias
            y_ref[r, cols] = (u[r] * s).astype(BF16)
    out = jnp.dot(y_ref[...], wout_ref[...], preferred_element_type=F32)
    o_ref[...] = x + _rms(out, gpost_ref[...])


def _sgu_call(x2d, gpre, w_in, ln_g, ln_b, w_s, b_st, w_out, gpost):
    t, d = x2d.shape
    rows = ROW_TILE
    const = lambda i: (0, 0)
    resident = pl.Buffered(1)
    return pl.pallas_call(
        _sgu_kernel,
        grid=(t // rows,),
        in_specs=[
            pl.BlockSpec((rows, d), lambda i: (i, 0)),
            pl.BlockSpec((1, d), const),
            pl.BlockSpec((d, 2 * SGU_WIDTH), const, pipeline_mode=resident),
            pl.BlockSpec((1, SGU_WIDTH), const),
            pl.BlockSpec((1, SGU_WIDTH), const),
            pl.BlockSpec((SGU_GROUPS, CHUNK, CHUNK), lambda i: (0, 0, 0)),
            pl.BlockSpec((CHUNK, SGU_GROUPS), const),
            pl.BlockSpec((SGU_WIDTH, d), const, pipeline_mode=resident),
            pl.BlockSpec((1, d), const),
        ],
        out_specs=pl.BlockSpec((rows, d), lambda i: (i, 0)),
        out_shape=jax.ShapeDtypeStruct((t, d), F32),
        scratch_shapes=[
            pltpu.VMEM((rows, d), BF16),
            pltpu.VMEM((rows, SGU_WIDTH), F32),
            pltpu.VMEM((rows, SGU_WIDTH), BF16),
            pltpu.VMEM((rows, SGU_WIDTH), BF16),
        ],
        compiler_params=pltpu.CompilerParams(
            dimension_semantics=("parallel",),
            vmem_limit_bytes=VMEM_LIMIT_BYTES),
        name="chunked_sgu",
    )(x2d, gpre, w_in, ln_g, ln_b, w_s, b_st, w_out, gpost)


def _conv_table(conv_w, conv_b):
    pad = jnp.zeros((SUBLANES - 4, FFN_DIM), F32)
    return jnp.concatenate([conv_w, conv_b[None, :], pad], axis=0)


def _row(v):
    return v.reshape(1, -1)


def kernel(x, attn_w_qkv, attn_lambda_q1, attn_lambda_k1, attn_lambda_q2, attn_lambda_k2, attn_subln, attn_w_o, sgu_w_in, sgu_ln_g, sgu_ln_b, sgu_w_s, sgu_b_s, sgu_w_out, norm_mix_pre, norm_mix_post, norm_ffn_pre, norm_ffn_post, ffn_w_up, ffn_conv_w, ffn_conv_b, ffn_w_down):
    b, s, d = x.shape
    assert d == D_MODEL and s % max(QKV_ROWS, ROW_TILE, ATTN_TQ) == 0 and ATTN_TQ == ATTN_TK

    w_qkv = attn_w_qkv[0]
    wqt = (w_qkv[:, :d].T * (HEAD_DIM ** -0.5)).astype(BF16)
    wk = w_qkv[:, d:2 * d].astype(BF16)
    wvt = w_qkv[:, 2 * d:].T.astype(BF16)
    qt, k, vt = _qkv_call(x, _row(norm_mix_pre[0]), wqt, wk, wvt)
    attn = _attn_call(qt, k, vt, _row(attn_lambda_q1[0]), _row(attn_lambda_k1[0]),
                      _row(attn_lambda_q2[0]), _row(attn_lambda_k2[0]), _row(attn_subln[0]))
    x2d = x.reshape(b * s, d)
    x2d = _oproj_call(x2d, attn.reshape(b * s, d), attn_w_o[0].astype(BF16),
                      _row(norm_mix_post[0]))
    x2d = _ffn_call(x2d, s, _row(norm_ffn_pre[0]), ffn_w_up[0].astype(BF16),
                    _conv_table(ffn_conv_w[0], ffn_conv_b[0]), ffn_w_down[0].astype(BF16),
                    _row(norm_ffn_post[0]))

    x2d = _sgu_call(x2d, _row(norm_mix_pre[1]), sgu_w_in[0].astype(BF16), _row(sgu_ln_g[0]),
                    _row(sgu_ln_b[0]), sgu_w_s[0], sgu_b_s[0].T, sgu_w_out[0].astype(BF16),
                    _row(norm_mix_post[1]))
    x2d = _ffn_call(x2d, s, _row(norm_ffn_pre[1]), ffn_w_up[1].astype(BF16),
                    _conv_table(ffn_conv_w[1], ffn_conv_b[1]), ffn_w_down[1].astype(BF16),
                    _row(norm_ffn_post[1]))
    return x2d.reshape(b, s, d)
```

```python
import functools
import math

import jax
import jax.numpy as jnp
from jax import lax
from jax.experimental import pallas as pl
from jax.experimental.pallas import tpu as pltpu

F32 = jnp.float32
BF16 = jnp.bfloat16

D_MODEL = 1024
ATTN_HEADS = 8
HEAD_DIM = 64
V_DIM = 2 * HEAD_DIM
SGU_WIDTH = 2 * D_MODEL
SGU_GROUPS = 8
SGU_GROUP_DIM = SGU_WIDTH // SGU_GROUPS
CHUNK = 128
FFN_DIM = 2816
NORM_EPS = 1e-6
LN_EPS = 1e-5
LAM_INIT = 0.8 - 0.6 * math.exp(-0.3 * 0)

MXU_DIM = 256
SUBLANES = 8
V_PAD = 16
V_AUG = V_DIM + V_PAD
LOG2E = math.log2(math.e)
VMEM_LIMIT_BYTES = 56 * 1024 * 1024

QKV_ROWS = 512
ATTN_TQ = 512
ATTN_TK = 512
ROW_TILE = 512
FFN_CHUNK = MXU_DIM
MASK_VALUE = -1e30
KNORM_ROWS = 1024
SKIP_MARGIN = 110.0
NORM_SLACK = 1.02

_NT_DIMS = (((1,), (1,)), ((), ()))


def _rms(x, gain):
    return x * lax.rsqrt(jnp.mean(x * x, axis=-1, keepdims=True) + NORM_EPS) * gain


def _gelu_tanh(x):
    return x * (0.5 * (1.0 + jnp.tanh(0.7978845608028654 * (x + 0.044715 * (x * x * x)))))


def _qkv_kernel(x_ref, g_ref, wqt_ref, wk_ref, wvt_ref, qt_ref, k_ref, vt_ref):
    rows = x_ref.shape[1]
    hn = _rms(x_ref[0], g_ref[...]).astype(BF16)
    k_ref[0] = jnp.dot(hn, wk_ref[...], preferred_element_type=F32).astype(BF16)
    qt_ref[0] = lax.dot_general(wqt_ref[...], hn, _NT_DIMS,
                                preferred_element_type=F32).astype(BF16)
    vt = lax.dot_general(wvt_ref[...], hn, _NT_DIMS,
                         preferred_element_type=F32).astype(BF16)
    pad_row = lax.broadcasted_iota(jnp.int32, (V_PAD, rows), 0)
    ones_rows = jnp.where(pad_row == 0, 1.0, 0.0).astype(BF16)
    for h in range(ATTN_HEADS):
        vt_ref[0, h * V_AUG:h * V_AUG + V_DIM, :] = vt[h * V_DIM:(h + 1) * V_DIM]
        vt_ref[0, h * V_AUG + V_DIM:(h + 1) * V_AUG, :] = ones_rows


def _qkv_call(x, gain, wqt, wk, wvt):
    b, s, d = x.shape
    rows = QKV_ROWS
    const = lambda bi, i: (0, 0)
    return pl.pallas_call(
        _qkv_kernel,
        grid=(b, s // rows),
        in_specs=[
            pl.BlockSpec((1, rows, d), lambda bi, i: (bi, i, 0)),
            pl.BlockSpec((1, d), const),
            pl.BlockSpec((d, d), const),
            pl.BlockSpec((d, d), const),
            pl.BlockSpec((d, d), const),
        ],
        out_specs=[
            pl.BlockSpec((1, d, rows), lambda bi, i: (bi, 0, i)),
            pl.BlockSpec((1, rows, d), lambda bi, i: (bi, i, 0)),
            pl.BlockSpec((1, ATTN_HEADS * V_AUG, rows), lambda bi, i: (bi, 0, i)),
        ],
        out_shape=[
            jax.ShapeDtypeStruct((b, d, s), BF16),
            jax.ShapeDtypeStruct((b, s, d), BF16),
            jax.ShapeDtypeStruct((b, ATTN_HEADS * V_AUG, s), BF16),
        ],
        compiler_params=pltpu.CompilerParams(
            dimension_semantics=("parallel", "parallel"),
            vmem_limit_bytes=VMEM_LIMIT_BYTES),
        name="qkv_proj",
    )(x, gain, wqt, wk, wvt)


def _attn_kernel(shift_ref, qt_ref, k_ref, vt_ref, bias_ref, lq1_ref, lk1_ref, lq2_ref,
                 lk2_ref, subln_ref, o_ref, w_ref, acc_ref, s_a, s_b, p_a, p_b, kmax_ref):
    tq, tk = ATTN_TQ, ATTN_TK
    buf_a = (s_a, p_a)
    buf_b = (s_b, p_b)
    head = pl.program_id(1)
    qi = pl.program_id(2)
    shift = shift_ref[0, head]
    inv_slope = shift_ref[1, head]

    @pl.when(qi == 0)
    def _():
        dim = lax.broadcasted_iota(jnp.int32, (V_DIM, V_DIM), 0)
        col = lax.broadcasted_iota(jnp.int32, (V_DIM, V_DIM), 1)
        map_of_dim = jnp.where(dim >= HEAD_DIM, 1, 0)
        select = jnp.where(col == map_of_dim, 1.0, 0.0).astype(BF16)

        def chunk_max(c, best):
            kk = k_ref[0, pl.ds(pl.multiple_of(c * KNORM_ROWS, KNORM_ROWS), KNORM_ROWS), :]
            norms = jnp.dot(kk * kk, select, preferred_element_type=F32)
            return jnp.maximum(best, jnp.max(norms, axis=0, keepdims=True))

        kmax_ref[...] = lax.fori_loop(0, k_ref.shape[1] // KNORM_ROWS, chunk_max,
                                      jnp.zeros((1, V_DIM), F32))

    qt = qt_ref[0]

    qf = qt.astype(F32)
    ktf = k_ref[0, pl.ds(pl.multiple_of(qi * tq, tq), tq), :].astype(F32).T
    bound = None
    for mi in range(2):
        rows = slice(mi * HEAD_DIM, (mi + 1) * HEAD_DIM)
        q_sq = jnp.max(jnp.sum(qf[rows] * qf[rows], axis=0, keepdims=True), axis=1, keepdims=True)
        self_min = jnp.min(jnp.sum(qf[rows] * ktf[rows], axis=0, keepdims=True), axis=1,
                           keepdims=True)
        k_sq = kmax_ref[:, mi:mi + 1]
        w_map = NORM_SLACK * jnp.sqrt(q_sq * k_sq) - self_min + SKIP_MARGIN
        bound = w_map if bound is None else jnp.maximum(bound, w_map)
    first_query = jnp.full((1, 1), qi * tq, jnp.int32).astype(F32)
    skip_f = jnp.floor((first_query - bound * inv_slope) * (1.0 / tk))
    j0 = jnp.clip(jnp.clip(skip_f, 0.0, float(2 ** 20)).astype(jnp.int32)[0, 0], 0,
                  qi * (tq // tk))

    zeros = jnp.zeros((HEAD_DIM, tq), BF16)
    row = lax.broadcasted_iota(jnp.int32, (V_DIM, tq), 0)
    ones = jnp.where(row < 2, 1.0, 0.0).astype(BF16)
    w_ref[:, 0:tq] = jnp.concatenate([qt[0:HEAD_DIM], zeros, ones], axis=0)
    w_ref[:, tq:2 * tq] = jnp.concatenate([zeros, qt[HEAD_DIM:V_DIM], ones], axis=0)
    acc_ref[...] = jnp.zeros(acc_ref.shape, F32)
    for _, p_buf in (buf_a, buf_b):
        p_buf[...] = jnp.zeros(p_buf.shape, BF16)

    def scores(j, buf):
        start = pl.multiple_of(j * tk, tk)
        k_aug = jnp.concatenate([k_ref[0, pl.ds(start, tk), :], bias_ref[0]], axis=1)
        buf[0][...] = jnp.dot(k_aug, w_ref[...], preferred_element_type=F32)

    def softmax(buf, diag, m, token):
        s_buf, p_buf = buf
        m_out, alpha_out = [], []
        for mi in range(2):
            sm = s_buf[:, mi * tq:(mi + 1) * tq]
            if diag is not None:
                key_pos = lax.broadcasted_iota(jnp.int32, (tk, tq), 0) + diag * tk
                qry_pos = lax.broadcasted_iota(jnp.int32, (tk, tq), 1)
                sm = jnp.where(key_pos <= qry_pos, sm, MASK_VALUE)
            sm3 = sm.reshape(tk // SUBLANES, SUBLANES, tq)
            blk_max = jnp.max(sm3, axis=0)
            for rot in (4, 2, 1):
                blk_max = jnp.maximum(blk_max, pltpu.roll(blk_max, rot, 0))
            m_prev = m[mi] - shift
            m_new = jnp.maximum(m_prev, blk_max)
            alpha_out.append(jnp.exp(m_prev - m_new))
            m_out.append(m_new)
            arg = ((sm3 - (m_new + 0.0 * token[mi])[None]) * LOG2E).reshape(tk, tq)
            p_buf[mi] = jnp.exp2(arg.astype(BF16))
        return tuple(m_out), tuple(alpha_out)

    def pv_update(j, buf, alpha):
        _, p_buf = buf
        start = pl.multiple_of(j * tk, tk)
        vt = vt_ref[0, :, pl.ds(start, tk)]
        token = []
        for mi in range(2):
            pv = jnp.dot(vt, p_buf[mi], preferred_element_type=F32)
            acc3 = acc_ref[mi].reshape(V_AUG // SUBLANES, SUBLANES, tq)
            acc_ref[mi] = (alpha[mi][None] * acc3).reshape(V_AUG, tq) + pv
            token.append(pv[V_DIM:V_DIM + SUBLANES])
        return tuple(token)

    def step(j, cur, nxt, state, diag=None, prefetch=True):
        m, alpha_prev, token_cur = state
        if prefetch:
            scores(j + 1, nxt)
        token_nxt = pv_update(jnp.maximum(j - 1, 0), nxt, alpha_prev)
        m, alpha = softmax(cur, diag, m, token_cur)
        return m, alpha, token_nxt

    def finish(cur, nxt, state):
        for d in range(n_diag):
            state = step(first_diag + d, cur, nxt, state, diag=d, prefetch=d + 1 < n_diag)
            cur, nxt = nxt, cur
        pv_update(first_diag + n_diag - 1, nxt, state[1])

    n_diag = tq // tk
    first_diag = qi * n_diag
    n_full = first_diag - j0
    scores(j0, buf_a)

    def two_steps(t, state):
        state = step(j0 + 2 * t, buf_a, buf_b, state)
        return step(j0 + 2 * t + 1, buf_b, buf_a, state)

    row_init = lambda v: (jnp.full((SUBLANES, tq), v, F32), jnp.full((SUBLANES, tq), v, F32))
    state = lax.fori_loop(0, n_full // 2, two_steps,
                          (row_init(MASK_VALUE), row_init(1.0), row_init(0.0)))

    @pl.when(n_full % 2 == 1)
    def _():
        finish(buf_b, buf_a, step(first_diag - 1, buf_a, buf_b, state))

    @pl.when(n_full % 2 == 0)
    def _():
        finish(buf_a, buf_b, state)

    lam = (jnp.exp(jnp.sum(lq1_ref[...] * lk1_ref[...], keepdims=True))
           - jnp.exp(jnp.sum(lq2_ref[...] * lk2_ref[...], keepdims=True)) + LAM_INIT)
    o1 = acc_ref[0, 0:V_DIM, :] * (1.0 / acc_ref[0, V_DIM:V_DIM + 1, :])
    o2 = acc_ref[1, 0:V_DIM, :] * (1.0 / acc_ref[1, V_DIM:V_DIM + 1, :])
    ot = o1 - lam * o2
    o = ot.T
    o_ref[0] = (_rms(o, subln_ref[...]) * (1.0 - LAM_INIT)).astype(BF16)


def _alibi_tables():
    slopes = [2.0 ** (-8.0 * (h + 1) / ATTN_HEADS) for h in range(ATTN_HEADS)]
    pos = jnp.arange(ATTN_TK)
    low = (pos % MXU_DIM).astype(F32)
    high = (pos - pos % MXU_DIM).astype(F32)
    tile = jnp.zeros((ATTN_HEADS, ATTN_TK, V_DIM), F32)
    sl = jnp.asarray(slopes, F32)[:, None]
    tile = tile.at[:, :, 0].set(sl * low[None, :])
    tile = tile.at[:, :, 1].set(sl * high[None, :])
    scalars = jnp.asarray([[s * ATTN_TK for s in slopes], [1.0 / s for s in slopes]], F32)
    return tile.astype(BF16), scalars


def _attn_call(qt, k, vt, lq1, lk1, lq2, lk2, subln):
    b, s, d = k.shape
    tq, tk = ATTN_TQ, ATTN_TK
    bias, shifts = _alibi_tables()
    vec = lambda n: pl.BlockSpec((1, n), lambda bi, h, i: (0, 0))
    return pl.pallas_call(
        _attn_kernel,
        grid=(b, ATTN_HEADS, s // tq),
        in_specs=[
            pl.BlockSpec(memory_space=pltpu.SMEM),
            pl.BlockSpec((1, V_DIM, tq), lambda bi, h, i: (bi, h, i)),
            pl.BlockSpec((1, s, V_DIM), lambda bi, h, i: (bi, 0, h)),
            pl.BlockSpec((1, V_AUG, s), lambda bi, h, i: (bi, h, 0)),
            pl.BlockSpec((1, tk, V_DIM), lambda bi, h, i: (h, 0, 0)),
            vec(HEAD_DIM), vec(HEAD_DIM), vec(HEAD_DIM), vec(HEAD_DIM),
            vec(V_DIM),
        ],
        out_specs=pl.BlockSpec((1, tq, V_DIM), lambda bi, h, i: (bi, i, h)),
        out_shape=jax.ShapeDtypeStruct((b, s, d), BF16),
        scratch_shapes=[
            pltpu.VMEM((2 * V_DIM, 2 * tq), BF16),
            pltpu.VMEM((2, V_AUG, tq), F32),
            pltpu.VMEM((tk, 2 * tq), F32),
            pltpu.VMEM((tk, 2 * tq), F32),
            pltpu.VMEM((2, tk, tq), BF16),
            pltpu.VMEM((2, tk, tq), BF16),
            pltpu.VMEM((1, V_DIM), F32),
        ],
        compiler_params=pltpu.CompilerParams(
            dimension_semantics=("parallel", "parallel", "arbitrary"),
            vmem_limit_bytes=VMEM_LIMIT_BYTES),
        name="diff_attn",
    )(shifts, qt, k, vt, bias, lq1, lk1, lq2, lk2, subln)


def _oproj_kernel(x_ref, a_ref, w_ref, g_ref, o_ref):
    m = jnp.dot(a_ref[...], w_ref[...], preferred_element_type=F32)
    o_ref[...] = x_ref[...] + _rms(m, g_ref[...])


def _oproj_call(x2d, a2d, w, gain):
    t, d = x2d.shape
    rows = ROW_TILE
    return pl.pallas_call(
        _oproj_kernel,
        grid=(t // rows,),
        in_specs=[
            pl.BlockSpec((rows, d), lambda i: (i, 0)),
            pl.BlockSpec((rows, d), lambda i: (i, 0)),
            pl.BlockSpec((d, d), lambda i: (0, 0)),
            pl.BlockSpec((1, d), lambda i: (0, 0)),
        ],
        out_specs=pl.BlockSpec((rows, d), lambda i: (i, 0)),
        out_shape=jax.ShapeDtypeStruct((t, d), F32),
        compiler_params=pltpu.CompilerParams(
            dimension_semantics=("parallel",),
            vmem_limit_bytes=VMEM_LIMIT_BYTES),
        name="attn_oproj",
    )(x2d, a2d, w, gain)


def _ffn_kernel(x_ref, gpre_ref, wa_ref, wg_ref, cw_ref, wd_ref, gpost_ref, o_ref,
                hn_ref, abuf_ref, carry_ref, act_ref, *, tiles_per_seq):
    rows = x_ref.shape[0]
    seq_start = (pl.program_id(0) % tiles_per_seq) == 0
    x = x_ref[...]
    hn_ref[...] = _rms(x, gpre_ref[...]).astype(BF16)
    for c in range(FFN_DIM // FFN_CHUNK):
        cols = slice(c * FFN_CHUNK, (c + 1) * FFN_CHUNK)
        a = jnp.dot(hn_ref[...], wa_ref[:, cols], preferred_element_type=F32)
        abuf_ref[0:SUBLANES, :] = jnp.where(seq_start, 0.0, carry_ref[c])
        abuf_ref[SUBLANES:SUBLANES + rows, :] = a
        carry_ref[c] = a[rows - SUBLANES:rows, :]
        cw = cw_ref[:, cols]
        a_conv = (cw[3:4] + abuf_ref[SUBLANES - 2:SUBLANES - 2 + rows, :] * cw[0:1]
                  + abuf_ref[SUBLANES - 1:SUBLANES - 1 + rows, :] * cw[1:2] + a * cw[2:3])
        g = jnp.dot(hn_ref[...], wg_ref[:, cols], preferred_element_type=F32)
        act_ref[:, cols] = (_gelu_tanh(a_conv) * g).astype(BF16)
    f = jnp.dot(act_ref[...], wd_ref[...], preferred_element_type=F32)
    o_ref[...] = x + _rms(f, gpost_ref[...])


def _ffn_call(x2d, seq_len, gpre, w_up, conv_tab, w_down, gpost):
    t, d = x2d.shape
    rows = ROW_TILE
    const = lambda i: (0, 0)
    resident = pl.Buffered(1)
    return pl.pallas_call(
        functools.partial(_ffn_kernel, tiles_per_seq=seq_len // rows),
        grid=(t // rows,),
        in_specs=[
            pl.BlockSpec((rows, d), lambda i: (i, 0)),
            pl.BlockSpec((1, d), const),
            pl.BlockSpec((d, FFN_DIM), lambda i: (0, 0), pipeline_mode=resident),
            pl.BlockSpec((d, FFN_DIM), lambda i: (0, 1), pipeline_mode=resident),
            pl.BlockSpec((SUBLANES, FFN_DIM), const),
            pl.BlockSpec((FFN_DIM, d), const, pipeline_mode=resident),
            pl.BlockSpec((1, d), const),
        ],
        out_specs=pl.BlockSpec((rows, d), lambda i: (i, 0)),
        out_shape=jax.ShapeDtypeStruct((t, d), F32),
        scratch_shapes=[
            pltpu.VMEM((rows, d), BF16),
            pltpu.VMEM((rows + SUBLANES, FFN_CHUNK), F32),
            pltpu.VMEM((FFN_DIM // FFN_CHUNK, SUBLANES, FFN_CHUNK), F32),
            pltpu.VMEM((rows, FFN_DIM), BF16),
        ],
        compiler_params=pltpu.CompilerParams(
            dimension_semantics=("arbitrary",),
            vmem_limit_bytes=VMEM_LIMIT_BYTES),
        name="conv_ffn",
    )(x2d, gpre, w_up, w_up, conv_tab, w_down, gpost)


def _sgu_kernel(x_ref, gpre_ref, win_ref, lng_ref, lnb_ref, ws_ref, bst_ref, wout_ref,
                gpost_ref, o_ref, hn_ref, v_ref, vn_ref, y_ref):
    rows = x_ref.shape[0]
    gd = SGU_GROUP_DIM
    x = x_ref[...]
    hn_ref[...] = _rms(x, gpre_ref[...]).astype(BF16)
    for g in range(SGU_GROUPS):
        cols = slice(SGU_WIDTH + g * gd, SGU_WIDTH + (g + 1) * gd)
        v_ref[:, g * gd:(g + 1) * gd] = _gelu_tanh(
            jnp.dot(hn_ref[...], win_ref[:, cols], preferred_element_type=F32))
    v = v_ref[...]
    mu = jnp.mean(v, axis=-1, keepdims=True)
    vc = v - mu
    var = jnp.mean(vc * vc, axis=-1, keepdims=True)
    vn_ref[...] = (vc * lax.rsqrt(var + LN_EPS) * lng_ref[...] + lnb_ref[...]).astype(BF16)
    t_pos = lax.broadcasted_iota(jnp.int32, (CHUNK, CHUNK), 0)
    s_pos = lax.broadcasted_iota(jnp.int32, (CHUNK, CHUNK), 1)
    for g in range(SGU_GROUPS):
        cols = slice(g * gd, (g + 1) * gd)
        u = _gelu_tanh(jnp.dot(hn_ref[...], win_ref[:, cols], preferred_element_type=F32))
        w = jnp.where(s_pos <= t_pos, ws_ref[g], 0.0).astype(BF16)
        bias = bst_ref[:, g:g + 1]
        for c in range(rows // CHUNK):
            r = slice(c * CHUNK, (c + 1) * CHUNK)
            s = jnp.dot(w, vn_ref[r, cols], preferred_element_type=F32) + bias
            y_ref[r, cols] = (u[r] * s).astype(BF16)
    out = jnp.dot(y_ref[...], wout_ref[...], preferred_element_type=F32)
    o_ref[...] = x + _rms(out, gpost_ref[...])


def _sgu_call(x2d, gpre, w_in, ln_g, ln_b, w_s, b_st, w_out, gpost):
    t, d = x2d.shape
    rows = ROW_TILE
    const = lambda i: (0, 0)
    resident = pl.Buffered(1)
    return pl.pallas_call(
        _sgu_kernel,
        grid=(t // rows,),
        in_specs=[
            pl.BlockSpec((rows, d), lambda i: (i, 0)),
            pl.BlockSpec((1, d), const),
            pl.BlockSpec((d, 2 * SGU_WIDTH), const, pipeline_mode=resident),
            pl.BlockSpec((1, SGU_WIDTH), const),
            pl.BlockSpec((1, SGU_WIDTH), const),
            pl.BlockSpec((SGU_GROUPS, CHUNK, CHUNK), lambda i: (0, 0, 0)),
            pl.BlockSpec((CHUNK, SGU_GROUPS), const),
            pl.BlockSpec((SGU_WIDTH, d), const, pipeline_mode=resident),
            pl.BlockSpec((1, d), const),
        ],
        out_specs=pl.BlockSpec((rows, d), lambda i: (i, 0)),
        out_shape=jax.ShapeDtypeStruct((t, d), F32),
        scratch_shapes=[
            pltpu.VMEM((rows, d), BF16),
            pltpu.VMEM((rows, SGU_WIDTH), F32),
            pltpu.VMEM((rows, SGU_WIDTH), BF16),
            pltpu.VMEM((rows, SGU_WIDTH), BF16),
        ],
        compiler_params=pltpu.CompilerParams(
            dimension_semantics=("parallel",),
            vmem_limit_bytes=VMEM_LIMIT_BYTES),
        name="chunked_sgu",
    )(x2d, gpre, w_in, ln_g, ln_b, w_s, b_st, w_out, gpost)


def _conv_table(conv_w, conv_b):
    pad = jnp.zeros((SUBLANES - 4, FFN_DIM), F32)
    return jnp.concatenate([conv_w, conv_b[None, :], pad], axis=0)


def _row(v):
    return v.reshape(1, -1)


def kernel(x, attn_w_qkv, attn_lambda_q1, attn_lambda_k1, attn_lambda_q2, attn_lambda_k2, attn_subln, attn_w_o, sgu_w_in, sgu_ln_g, sgu_ln_b, sgu_w_s, sgu_b_s, sgu_w_out, norm_mix_pre, norm_mix_post, norm_ffn_pre, norm_ffn_post, ffn_w_up, ffn_conv_w, ffn_conv_b, ffn_w_down):
    b, s, d = x.shape
    assert d == D_MODEL and s % max(QKV_ROWS, ROW_TILE, ATTN_TQ) == 0 and ATTN_TQ % ATTN_TK == 0

    w_qkv = attn_w_qkv[0]
    wqt = (w_qkv[:, :d].T * (HEAD_DIM ** -0.5)).astype(BF16)
    wk = w_qkv[:, d:2 * d].astype(BF16)
    wvt = w_qkv[:, 2 * d:].T.astype(BF16)
    qt, k, vt = _qkv_call(x, _row(norm_mix_pre[0]), wqt, wk, wvt)
    attn = _attn_call(qt, k, vt, _row(attn_lambda_q1[0]), _row(attn_lambda_k1[0]),
                      _row(attn_lambda_q2[0]), _row(attn_lambda_k2[0]), _row(attn_subln[0]))
    x2d = x.reshape(b * s, d)
    x2d = _oproj_call(x2d, attn.reshape(b * s, d), attn_w_o[0].astype(BF16),
                      _row(norm_mix_post[0]))
    x2d = _ffn_call(x2d, s, _row(norm_ffn_pre[0]), ffn_w_up[0].astype(BF16),
                    _conv_table(ffn_conv_w[0], ffn_conv_b[0]), ffn_w_down[0].astype(BF16),
                    _row(norm_ffn_post[0]))

    x2d = _sgu_call(x2d, _row(norm_mix_pre[1]), sgu_w_in[0].astype(BF16), _row(sgu_ln_g[0]),
                    _row(sgu_ln_b[0]), sgu_w_s[0], sgu_b_s[0].T, sgu_w_out[0].astype(BF16),
                    _row(norm_mix_post[1]))
    x2d = _ffn_call(x2d, s, _row(norm_ffn_pre[1]), ffn_w_up[1].astype(BF16),
                    _conv_table(ffn_conv_w[1], ffn_conv_b[1]), ffn_w_down[1].astype(BF16),
                    _row(norm_ffn_post[1]))
    return x2d.reshape(b, s, d)
```

```python
import functools
import math

import jax
import jax.numpy as jnp
import numpy as np
from jax import lax
from jax.experimental import pallas as pl
from jax.experimental.pallas import tpu as pltpu

F32 = jnp.float32
BF16 = jnp.bfloat16

D_MODEL = 1024
ATTN_HEADS = 8
HEAD_DIM = 64
V_DIM = 2 * HEAD_DIM
SGU_WIDTH = 2 * D_MODEL
SGU_GROUPS = 8
SGU_GROUP_DIM = SGU_WIDTH // SGU_GROUPS
CHUNK = 128
FFN_DIM = 2816
NORM_EPS = 1e-6
LN_EPS = 1e-5
LAM_INIT = 0.8 - 0.6 * math.exp(-0.3 * 0)

MXU_DIM = 256
SUBLANES = 8
V_PAD = 16
V_AUG = V_DIM + V_PAD
LOG2E = math.log2(math.e)
BIAS_TERMS = 4
VMEM_LIMIT_BYTES = 56 * 1024 * 1024

QKV_ROWS = 512
ATTN_TQ = 512
ATTN_TK = 512
ROW_TILE = 512
FFN_CHUNK = MXU_DIM
MASK_VALUE = -1e30
KNORM_ROWS = 1024
SKIP_MARGIN = 110.0
NORM_SLACK = 1.02

_NT_DIMS = (((1,), (1,)), ((), ()))


def _rms(x, gain):
    return x * lax.rsqrt(jnp.mean(x * x, axis=-1, keepdims=True) + NORM_EPS) * gain


def _gelu_tanh(x):
    return x * (0.5 * (1.0 + jnp.tanh(0.7978845608028654 * (x + 0.044715 * (x * x * x)))))


def _qkv_kernel(x_ref, g_ref, wqt_ref, wk_ref, wvt_ref, qt_ref, k_ref, vt_ref):
    rows = x_ref.shape[1]
    hn = _rms(x_ref[0], g_ref[...]).astype(BF16)
    k_ref[0] = jnp.dot(hn, wk_ref[...], preferred_element_type=F32).astype(BF16)
    qt_ref[0] = lax.dot_general(wqt_ref[...], hn, _NT_DIMS,
                                preferred_element_type=F32).astype(BF16)
    vt = lax.dot_general(wvt_ref[...], hn, _NT_DIMS,
                         preferred_element_type=F32).astype(BF16)
    pad_row = lax.broadcasted_iota(jnp.int32, (V_PAD, rows), 0)
    ones_rows = jnp.where(pad_row == 0, 1.0, 0.0).astype(BF16)
    for h in range(ATTN_HEADS):
        vt_ref[0, h * V_AUG:h * V_AUG + V_DIM, :] = vt[h * V_DIM:(h + 1) * V_DIM]
        vt_ref[0, h * V_AUG + V_DIM:(h + 1) * V_AUG, :] = ones_rows


def _qkv_call(x, gain, wqt, wk, wvt):
    b, s, d = x.shape
    rows = QKV_ROWS
    const = lambda bi, i: (0, 0)
    return pl.pallas_call(
        _qkv_kernel,
        grid=(b, s // rows),
        in_specs=[
            pl.BlockSpec((1, rows, d), lambda bi, i: (bi, i, 0)),
            pl.BlockSpec((1, d), const),
            pl.BlockSpec((d, d), const),
            pl.BlockSpec((d, d), const),
            pl.BlockSpec((d, d), const),
        ],
        out_specs=[
            pl.BlockSpec((1, d, rows), lambda bi, i: (bi, 0, i)),
            pl.BlockSpec((1, rows, d), lambda bi, i: (bi, i, 0)),
            pl.BlockSpec((1, ATTN_HEADS * V_AUG, rows), lambda bi, i: (bi, 0, i)),
        ],
        out_shape=[
            jax.ShapeDtypeStruct((b, d, s), BF16),
            jax.ShapeDtypeStruct((b, s, d), BF16),
            jax.ShapeDtypeStruct((b, ATTN_HEADS * V_AUG, s), BF16),
        ],
        compiler_params=pltpu.CompilerParams(
            dimension_semantics=("parallel", "parallel"),
            vmem_limit_bytes=VMEM_LIMIT_BYTES),
        name="qkv_proj",
    )(x, gain, wqt, wk, wvt)


def _attn_kernel(shift_ref, qt_ref, k_ref, vt_ref, bias_ref, lq1_ref, lk1_ref, lq2_ref,
                 lk2_ref, subln_ref, o_ref, w_ref, acc_ref, s_a, s_b, p_a, p_b, kmax_ref):
    tq, tk = ATTN_TQ, ATTN_TK
    buf_a = (s_a, p_a)
    buf_b = (s_b, p_b)
    head = pl.program_id(1)
    qi = pl.program_id(2)
    shift = shift_ref[0, head]
    inv_slope = shift_ref[1, head]

    @pl.when(qi == 0)
    def _():
        dim = lax.broadcasted_iota(jnp.int32, (V_DIM, V_DIM), 0)
        col = lax.broadcasted_iota(jnp.int32, (V_DIM, V_DIM), 1)
        map_of_dim = jnp.where(dim >= HEAD_DIM, 1, 0)
        select = jnp.where(col == map_of_dim, 1.0, 0.0).astype(BF16)

        def chunk_max(c, best):
            kk = k_ref[0, pl.ds(pl.multiple_of(c * KNORM_ROWS, KNORM_ROWS), KNORM_ROWS), :]
            norms = jnp.dot(kk * kk, select, preferred_element_type=F32)
            return jnp.maximum(best, jnp.max(norms, axis=0, keepdims=True))

        kmax_ref[...] = lax.fori_loop(0, k_ref.shape[1] // KNORM_ROWS, chunk_max,
                                      jnp.zeros((1, V_DIM), F32))

    qt = qt_ref[0]

    qf = qt.astype(F32)
    ktf = k_ref[0, pl.ds(pl.multiple_of(qi * tq, tq), tq), :].astype(F32).T
    bound = None
    for mi in range(2):
        rows = slice(mi * HEAD_DIM, (mi + 1) * HEAD_DIM)
        q_sq = jnp.max(jnp.sum(qf[rows] * qf[rows], axis=0, keepdims=True), axis=1, keepdims=True)
        self_min = jnp.min(jnp.sum(qf[rows] * ktf[rows], axis=0, keepdims=True), axis=1,
                           keepdims=True)
        k_sq = kmax_ref[:, mi:mi + 1]
        w_map = NORM_SLACK * jnp.sqrt(q_sq * k_sq) - self_min + SKIP_MARGIN * LOG2E
        bound = w_map if bound is None else jnp.maximum(bound, w_map)
    first_query = jnp.full((1, 1), qi * tq, jnp.int32).astype(F32)
    skip_f = jnp.floor((first_query - bound * inv_slope) * (1.0 / tk))
    j0 = jnp.clip(jnp.clip(skip_f, 0.0, float(2 ** 20)).astype(jnp.int32)[0, 0], 0,
                  qi * (tq // tk))

    zeros = jnp.zeros((HEAD_DIM, tq), BF16)
    row = lax.broadcasted_iota(jnp.int32, (V_DIM, tq), 0)
    ones = jnp.where(row < BIAS_TERMS, 1.0, 0.0).astype(BF16)
    w_ref[:, 0:tq] = jnp.concatenate([qt[0:HEAD_DIM], zeros, ones], axis=0)
    w_ref[:, tq:2 * tq] = jnp.concatenate([zeros, qt[HEAD_DIM:V_DIM], ones], axis=0)
    acc_ref[...] = jnp.zeros(acc_ref.shape, F32)
    for _, p_buf in (buf_a, buf_b):
        p_buf[...] = jnp.zeros(p_buf.shape, BF16)

    def scores(j, buf):
        start = pl.multiple_of(j * tk, tk)
        k_aug = jnp.concatenate([k_ref[0, pl.ds(start, tk), :], bias_ref[0]], axis=1)
        buf[0][...] = jnp.dot(k_aug, w_ref[...], preferred_element_type=F32)

    def softmax(buf, diag, m, token):
        s_buf, p_buf = buf
        m_out, alpha_out = [], []
        for mi in range(2):
            sm = s_buf[:, mi * tq:(mi + 1) * tq]
            if diag is not None:
                key_pos = lax.broadcasted_iota(jnp.int32, (tk, tq), 0) + diag * tk
                qry_pos = lax.broadcasted_iota(jnp.int32, (tk, tq), 1)
                sm = jnp.where(key_pos <= qry_pos, sm, MASK_VALUE)
            sm3 = sm.reshape(tk // SUBLANES, SUBLANES, tq)
            blk_max = jnp.max(sm3, axis=0)
            for rot in (4, 2, 1):
                blk_max = jnp.maximum(blk_max, pltpu.roll(blk_max, rot, 0))
            m_prev = m[mi] - shift
            m_new = jnp.maximum(m_prev, blk_max)
            alpha_out.append(jnp.exp2(m_prev - m_new))
            m_out.append(m_new)
            arg = (sm3 - (m_new + 0.0 * token[mi])[None]).reshape(tk, tq)
            p_buf[mi] = jnp.exp2(arg.astype(BF16))
        return tuple(m_out), tuple(alpha_out)

    def pv_update(j, buf, alpha):
        _, p_buf = buf
        start = pl.multiple_of(j * tk, tk)
        vt = vt_ref[0, :, pl.ds(start, tk)]
        token = []
        for mi in range(2):
            pv = jnp.dot(vt, p_buf[mi], preferred_element_type=F32)
            acc3 = acc_ref[mi].reshape(V_AUG // SUBLANES, SUBLANES, tq)
            acc_ref[mi] = (alpha[mi][None] * acc3).reshape(V_AUG, tq) + pv
            token.append(pv[V_DIM:V_DIM + SUBLANES])
        return tuple(token)

    def step(j, cur, nxt, state, diag=None, prefetch=True):
        m, alpha_prev, token_cur = state
        if prefetch:
            scores(j + 1, nxt)
        token_nxt = pv_update(jnp.maximum(j - 1, 0), nxt, alpha_prev)
        m, alpha = softmax(cur, diag, m, token_cur)
        return m, alpha, token_nxt

    def finish(cur, nxt, state):
        for d in range(n_diag):
            state = step(first_diag + d, cur, nxt, state, diag=d, prefetch=d + 1 < n_diag)
            cur, nxt = nxt, cur
        pv_update(first_diag + n_diag - 1, nxt, state[1])

    n_diag = tq // tk
    first_diag = qi * n_diag
    n_full = first_diag - j0
    scores(j0, buf_a)

    def two_steps(t, state):
        state = step(j0 + 2 * t, buf_a, buf_b, state)
        return step(j0 + 2 * t + 1, buf_b, buf_a, state)

    row_init = lambda v: (jnp.full((SUBLANES, tq), v, F32), jnp.full((SUBLANES, tq), v, F32))
    state = lax.fori_loop(0, n_full // 2, two_steps,
                          (row_init(MASK_VALUE), row_init(1.0), row_init(0.0)))

    @pl.when(n_full % 2 == 1)
    def _():
        finish(buf_b, buf_a, step(first_diag - 1, buf_a, buf_b, state))

    @pl.when(n_full % 2 == 0)
    def _():
        finish(buf_a, buf_b, state)

    lam = (jnp.exp(jnp.sum(lq1_ref[...] * lk1_ref[...], keepdims=True))
           - jnp.exp(jnp.sum(lq2_ref[...] * lk2_ref[...], keepdims=True)) + LAM_INIT)
    o1 = acc_ref[0, 0:V_DIM, :] * (1.0 / acc_ref[0, V_DIM:V_DIM + 1, :])
    o2 = acc_ref[1, 0:V_DIM, :] * (1.0 / acc_ref[1, V_DIM:V_DIM + 1, :])
    ot = o1 - lam * o2
    o = ot.T
    o_ref[0] = (_rms(o, subln_ref[...]) * (1.0 - LAM_INIT)).astype(BF16)


def _alibi_tables():
    slopes = np.asarray([2.0 ** (-8.0 * (h + 1) / ATTN_HEADS) for h in range(ATTN_HEADS)]) * LOG2E
    rest = slopes[:, None] * np.arange(ATTN_TK, dtype=np.float64)[None, :]
    tile = np.zeros((ATTN_HEADS, ATTN_TK, V_DIM), np.float32)
    for term in range(BIAS_TERMS):
        part = rest.astype(jnp.bfloat16)
        tile[:, :, term] = part.astype(np.float32)
        rest = rest - part.astype(np.float64)
    scalars = np.stack([slopes * ATTN_TK, 1.0 / slopes]).astype(np.float32)
    return jnp.asarray(tile, BF16), jnp.asarray(scalars)


def _attn_call(qt, k, vt, lq1, lk1, lq2, lk2, subln):
    b, s, d = k.shape
    tq, tk = ATTN_TQ, ATTN_TK
    bias, shifts = _alibi_tables()
    vec = lambda n: pl.BlockSpec((1, n), lambda bi, h, i: (0, 0))
    return pl.pallas_call(
        _attn_kernel,
        grid=(b, ATTN_HEADS, s // tq),
        in_specs=[
            pl.BlockSpec(memory_space=pltpu.SMEM),
            pl.BlockSpec((1, V_DIM, tq), lambda bi, h, i: (bi, h, i)),
            pl.BlockSpec((1, s, V_DIM), lambda bi, h, i: (bi, 0, h)),
            pl.BlockSpec((1, V_AUG, s), lambda bi, h, i: (bi, h, 0)),
            pl.BlockSpec((1, tk, V_DIM), lambda bi, h, i: (h, 0, 0)),
            vec(HEAD_DIM), vec(HEAD_DIM), vec(HEAD_DIM), vec(HEAD_DIM),
            vec(V_DIM),
        ],
        out_specs=pl.BlockSpec((1, tq, V_DIM), lambda bi, h, i: (bi, i, h)),
        out_shape=jax.ShapeDtypeStruct((b, s, d), BF16),
        scratch_shapes=[
            pltpu.VMEM((2 * V_DIM, 2 * tq), BF16),
            pltpu.VMEM((2, V_AUG, tq), F32),
            pltpu.VMEM((tk, 2 * tq), F32),
            pltpu.VMEM((tk, 2 * tq), F32),
            pltpu.VMEM((2, tk, tq), BF16),
            pltpu.VMEM((2, tk, tq), BF16),
            pltpu.VMEM((1, V_DIM), F32),
        ],
        compiler_params=pltpu.CompilerParams(
            dimension_semantics=("parallel", "parallel", "arbitrary"),
            vmem_limit_bytes=VMEM_LIMIT_BYTES),
        name="diff_attn",
    )(shifts, qt, k, vt, bias, lq1, lk1, lq2, lk2, subln)


def _oproj_kernel(x_ref, a_ref, w_ref, g_ref, o_ref):
    m = jnp.dot(a_ref[...], w_ref[...], preferred_element_type=F32)
    o_ref[...] = x_ref[...] + _rms(m, g_ref[...])


def _oproj_call(x2d, a2d, w, gain):
    t, d = x2d.shape
    rows = ROW_TILE
    return pl.pallas_call(
        _oproj_kernel,
        grid=(t // rows,),
        in_specs=[
            pl.BlockSpec((rows, d), lambda i: (i, 0)),
            pl.BlockSpec((rows, d), lambda i: (i, 0)),
            pl.BlockSpec((d, d), lambda i: (0, 0)),
            pl.BlockSpec((1, d), lambda i: (0, 0)),
        ],
        out_specs=pl.BlockSpec((rows, d), lambda i: (i, 0)),
        out_shape=jax.ShapeDtypeStruct((t, d), F32),
        compiler_params=pltpu.CompilerParams(
            dimension_semantics=("parallel",),
            vmem_limit_bytes=VMEM_LIMIT_BYTES),
        name="attn_oproj",
    )(x2d, a2d, w, gain)


def _ffn_kernel(x_ref, gpre_ref, wa_ref, wg_ref, cw_ref, wd_ref, gpost_ref, o_ref,
                hn_ref, abuf_ref, carry_ref, act_ref, *, tiles_per_seq):
    rows = x_ref.shape[0]
    seq_start = (pl.program_id(0) % tiles_per_seq) == 0
    x = x_ref[...]
    hn_ref[...] = _rms(x, gpre_ref[...]).astype(BF16)
    for c in range(FFN_DIM // FFN_CHUNK):
        cols = slice(c * FFN_CHUNK, (c + 1) * FFN_CHUNK)
        a = jnp.dot(hn_ref[...], wa_ref[:, cols], preferred_element_type=F32)
        abuf_ref[0:SUBLANES, :] = jnp.where(seq_start, 0.0, carry_ref[c])
        abuf_ref[SUBLANES:SUBLANES + rows, :] = a
        carry_ref[c] = a[rows - SUBLANES:rows, :]
        cw = cw_ref[:, cols]
        a_conv = (cw[3:4] + abuf_ref[SUBLANES - 2:SUBLANES - 2 + rows, :] * cw[0:1]
                  + abuf_ref[SUBLANES - 1:SUBLANES - 1 + rows, :] * cw[1:2] + a * cw[2:3])
        g = jnp.dot(hn_ref[...], wg_ref[:, cols], preferred_element_type=F32)
        act_ref[:, cols] = (_gelu_tanh(a_conv) * g).astype(BF16)
    f = jnp.dot(act_ref[...], wd_ref[...], preferred_element_type=F32)
    o_ref[...] = x + _rms(f, gpost_ref[...])


def _ffn_call(x2d, seq_len, gpre, w_up, conv_tab, w_down, gpost):
    t, d = x2d.shape
    rows = ROW_TILE
    const = lambda i: (0, 0)
    resident = pl.Buffered(1)
    return pl.pallas_call(
        functools.partial(_ffn_kernel, tiles_per_seq=seq_len // rows),
        grid=(t // rows,),
        in_specs=[
            pl.BlockSpec((rows, d), lambda i: (i, 0)),
            pl.BlockSpec((1, d), const),
            pl.BlockSpec((d, FFN_DIM), lambda i: (0, 0), pipeline_mode=resident),
            pl.BlockSpec((d, FFN_DIM), lambda i: (0, 1), pipeline_mode=resident),
            pl.BlockSpec((SUBLANES, FFN_DIM), const),
            pl.BlockSpec((FFN_DIM, d), const, pipeline_mode=resident),
            pl.BlockSpec((1, d), const),
        ],
        out_specs=pl.BlockSpec((rows, d), lambda i: (i, 0)),
        out_shape=jax.ShapeDtypeStruct((t, d), F32),
        scratch_shapes=[
            pltpu.VMEM((rows, d), BF16),
            pltpu.VMEM((rows + SUBLANES, FFN_CHUNK), F32),
            pltpu.VMEM((FFN_DIM // FFN_CHUNK, SUBLANES, FFN_CHUNK), F32),
            pltpu.VMEM((rows, FFN_DIM), BF16),
        ],
        compiler_params=pltpu.CompilerParams(
            dimension_semantics=("arbitrary",),
            vmem_limit_bytes=VMEM_LIMIT_BYTES),
        name="conv_ffn",
    )(x2d, gpre, w_up, w_up, conv_tab, w_down, gpost)


def _sgu_kernel(x_ref, gpre_ref, win_ref, lng_ref, lnb_ref, ws_ref, bst_ref, wout_ref,
                gpost_ref, o_ref, hn_ref, v_ref, vn_ref, y_ref):
    rows = x_ref.shape[0]
    gd = SGU_GROUP_DIM
    x = x_ref[...]
    hn_ref[...] = _rms(x, gpre_ref[...]).astype(BF16)
    for g in range(SGU_GROUPS):
        cols = slice(SGU_WIDTH + g * gd, SGU_WIDTH + (g + 1) * gd)
        v_ref[:, g * gd:(g + 1) * gd] = _gelu_tanh(
            jnp.dot(hn_ref[...], win_ref[:, cols], preferred_element_type=F32))
    v = v_ref[...]
    mu = jnp.mean(v, axis=-1, keepdims=True)
    vc = v - mu
    var = jnp.mean(vc * vc, axis=-1, keepdims=True)
    vn_ref[...] = (vc * lax.rsqrt(var + LN_EPS) * lng_ref[...] + lnb_ref[...]).astype(BF16)
    t_pos = lax.broadcasted_iota(jnp.int32, (CHUNK, CHUNK), 0)
    s_pos = lax.broadcasted_iota(jnp.int32, (CHUNK, CHUNK), 1)
    for g in range(SGU_GROUPS):
        cols = slice(g * gd, (g + 1) * gd)
        u = _gelu_tanh(jnp.dot(hn_ref[...], win_ref[:, cols], preferred_element_type=F32))
        w = jnp.where(s_pos <= t_pos, ws_ref[g], 0.0).astype(BF16)
        bias = bst_ref[:, g:g + 1]
        for c in range(rows // CHUNK):
            r = slice(c * CHUNK, (c + 1) * CHUNK)
            s = jnp.dot(w, vn_ref[r, cols], preferred_element_type=F32) + bias
            y_ref[r, cols] = (u[r] * s).astype(BF16)
    out = jnp.dot(y_ref[...], wout_ref[...], preferred_element_type=F32)
    o_ref[...] = x + _rms(out, gpost_ref[...])


def _sgu_call(x2d, gpre, w_in, ln_g, ln_b, w_s, b_st, w_out, gpost):
    t, d = x2d.shape
    rows = ROW_TILE
    const = lambda i: (0, 0)
    resident = pl.Buffered(1)
    return pl.pallas_call(
        _sgu_kernel,
        grid=(t // rows,),
        in_specs=[
            pl.BlockSpec((rows, d), lambda i: (i, 0)),
            pl.BlockSpec((1, d), const),
            pl.BlockSpec((d, 2 * SGU_WIDTH), const, pipeline_mode=resident),
            pl.BlockSpec((1, SGU_WIDTH), const),
            pl.BlockSpec((1, SGU_WIDTH), const),
            pl.BlockSpec((SGU_GROUPS, CHUNK, CHUNK), lambda i: (0, 0, 0)),
            pl.BlockSpec((CHUNK, SGU_GROUPS), const),
            pl.BlockSpec((SGU_WIDTH, d), const, pipeline_mode=resident),
            pl.BlockSpec((1, d), const),
        ],
        out_specs=pl.BlockSpec((rows, d), lambda i: (i, 0)),
        out_shape=jax.ShapeDtypeStruct((t, d), F32),
        scratch_shapes=[
            pltpu.VMEM((rows, d), BF16),
            pltpu.VMEM((rows, SGU_WIDTH), F32),
            pltpu.VMEM((rows, SGU_WIDTH), BF16),
            pltpu.VMEM((rows, SGU_WIDTH), BF16),
        ],
        compiler_params=pltpu.CompilerParams(
            dimension_semantics=("parallel",),
            vmem_limit_bytes=VMEM_LIMIT_BYTES),
        name="chunked_sgu",
    )(x2d, gpre, w_in, ln_g, ln_b, w_s, b_st, w_out, gpost)


def _conv_table(conv_w, conv_b):
    pad = jnp.zeros((SUBLANES - 4, FFN_DIM), F32)
    return jnp.concatenate([conv_w, conv_b[None, :], pad], axis=0)


def _row(v):
    return v.reshape(1, -1)


def kernel(x, attn_w_qkv, attn_lambda_q1, attn_lambda_k1, attn_lambda_q2, attn_lambda_k2, attn_subln, attn_w_o, sgu_w_in, sgu_ln_g, sgu_ln_b, sgu_w_s, sgu_b_s, sgu_w_out, norm_mix_pre, norm_mix_post, norm_ffn_pre, norm_ffn_post, ffn_w_up, ffn_conv_w, ffn_conv_b, ffn_w_down):
    b, s, d = x.shape
    assert d == D_MODEL and s % max(QKV_ROWS, ROW_TILE, ATTN_TQ) == 0 and ATTN_TQ % ATTN_TK == 0

    w_qkv = attn_w_qkv[0]
    wqt = (w_qkv[:, :d].T * (HEAD_DIM ** -0.5 * LOG2E)).astype(BF16)
    wk = w_qkv[:, d:2 * d].astype(BF16)
    wvt = w_qkv[:, 2 * d:].T.astype(BF16)
    qt, k, vt = _qkv_call(x, _row(norm_mix_pre[0]), wqt, wk, wvt)
    attn = _attn_call(qt, k, vt, _row(attn_lambda_q1[0]), _row(attn_lambda_k1[0]),
                      _row(attn_lambda_q2[0]), _row(attn_lambda_k2[0]), _row(attn_subln[0]))
    x2d = x.reshape(b * s, d)
    x2d = _oproj_call(x2d, attn.reshape(b * s, d), attn_w_o[0].astype(BF16),
                      _row(norm_mix_post[0]))
    x2d = _ffn_call(x2d, s, _row(norm_ffn_pre[0]), ffn_w_up[0].astype(BF16),
                    _conv_table(ffn_conv_w[0], ffn_conv_b[0]), ffn_w_down[0].astype(BF16),
                    _row(norm_ffn_post[0]))

    x2d = _sgu_call(x2d, _row(norm_mix_pre[1]), sgu_w_in[0].astype(BF16), _row(sgu_ln_g[0]),
                    _row(sgu_ln_b[0]), sgu_w_s[0], sgu_b_s[0].T, sgu_w_out[0].astype(BF16),
                    _row(norm_mix_post[1]))
    x2d = _ffn_call(x2d, s, _row(norm_ffn_pre[1]), ffn_w_up[1].astype(BF16),
                    _conv_table(ffn_conv_w[1], ffn_conv_b[1]), ffn_w_down[1].astype(BF16),
                    _row(norm_ffn_post[1]))
    return x2d.reshape(b, s, d)
```

```python
import functools
import math

import jax
import jax.numpy as jnp
import numpy as np
from jax import lax
from jax.experimental import pallas as pl
from jax.experimental.pallas import tpu as pltpu

F32 = jnp.float32
BF16 = jnp.bfloat16

D_MODEL = 1024
ATTN_HEADS = 8
HEAD_DIM = 64
V_DIM = 2 * HEAD_DIM
SGU_WIDTH = 2 * D_MODEL
SGU_GROUPS = 8
SGU_GROUP_DIM = SGU_WIDTH // SGU_GROUPS
CHUNK = 128
FFN_DIM = 2816
NORM_EPS = 1e-6
LN_EPS = 1e-5
LAM_INIT = 0.8 - 0.6 * math.exp(-0.3 * 0)

MXU_DIM = 256
SUBLANES = 8
V_PAD = 16
V_AUG = V_DIM + V_PAD
LOG2E = math.log2(math.e)
BIAS_TERMS = 4
VMEM_LIMIT_BYTES = 56 * 1024 * 1024

QKV_ROWS = 512
ATTN_TQ = 512
ATTN_TK = 512
ROW_TILE = 512
FFN_CHUNK = MXU_DIM
MASK_VALUE = -1e30
KNORM_ROWS = 1024
SKIP_MARGIN = 110.0
NORM_SLACK = 1.02

_NT_DIMS = (((1,), (1,)), ((), ()))


def _rms(x, gain):
    return x * lax.rsqrt(jnp.mean(x * x, axis=-1, keepdims=True) + NORM_EPS) * gain


def _gelu_tanh(x):
    return x * (0.5 * (1.0 + jnp.tanh(0.7978845608028654 * (x + 0.044715 * (x * x * x)))))


def _qkv_kernel(x_ref, g_ref, wqt_ref, wk_ref, wvt_ref, qt_ref, k_ref, vt_ref):
    rows = x_ref.shape[1]
    hn = _rms(x_ref[0], g_ref[...]).astype(BF16)
    k_ref[0] = jnp.dot(hn, wk_ref[...], preferred_element_type=F32).astype(BF16)
    qt_ref[0] = lax.dot_general(wqt_ref[...], hn, _NT_DIMS,
                                preferred_element_type=F32).astype(BF16)
    vt = lax.dot_general(wvt_ref[...], hn, _NT_DIMS,
                         preferred_element_type=F32).astype(BF16)
    pad_row = lax.broadcasted_iota(jnp.int32, (V_PAD, rows), 0)
    ones_rows = jnp.where(pad_row == 0, 1.0, 0.0).astype(BF16)
    for h in range(ATTN_HEADS):
        vt_ref[0, h * V_AUG:h * V_AUG + V_DIM, :] = vt[h * V_DIM:(h + 1) * V_DIM]
        vt_ref[0, h * V_AUG + V_DIM:(h + 1) * V_AUG, :] = ones_rows


def _qkv_call(x, gain, wqt, wk, wvt):
    b, s, d = x.shape
    rows = QKV_ROWS
    const = lambda bi, i: (0, 0)
    return pl.pallas_call(
        _qkv_kernel,
        grid=(b, s // rows),
        in_specs=[
            pl.BlockSpec((1, rows, d), lambda bi, i: (bi, i, 0)),
            pl.BlockSpec((1, d), const),
            pl.BlockSpec((d, d), const),
            pl.BlockSpec((d, d), const),
            pl.BlockSpec((d, d), const),
        ],
        out_specs=[
            pl.BlockSpec((1, d, rows), lambda bi, i: (bi, 0, i)),
            pl.BlockSpec((1, rows, d), lambda bi, i: (bi, i, 0)),
            pl.BlockSpec((1, ATTN_HEADS * V_AUG, rows), lambda bi, i: (bi, 0, i)),
        ],
        out_shape=[
            jax.ShapeDtypeStruct((b, d, s), BF16),
            jax.ShapeDtypeStruct((b, s, d), BF16),
            jax.ShapeDtypeStruct((b, ATTN_HEADS * V_AUG, s), BF16),
        ],
        compiler_params=pltpu.CompilerParams(
            dimension_semantics=("parallel", "parallel"),
            vmem_limit_bytes=VMEM_LIMIT_BYTES),
        name="qkv_proj",
    )(x, gain, wqt, wk, wvt)


def _attn_kernel(shift_ref, qt_ref, k_ref, vt_ref, bias_ref, lq1_ref, lk1_ref, lq2_ref,
                 lk2_ref, subln_ref, o_ref, w_ref, acc_ref, s_a, s_b, p_a, p_b, kmax_ref):
    tq, tk = ATTN_TQ, ATTN_TK
    buf_a = (s_a, p_a)
    buf_b = (s_b, p_b)
    head = pl.program_id(1)
    qi = pl.program_id(2)
    shift = shift_ref[0, head]
    inv_slope = shift_ref[1, head]

    @pl.when(qi == 0)
    def _():
        dim = lax.broadcasted_iota(jnp.int32, (V_DIM, V_DIM), 0)
        col = lax.broadcasted_iota(jnp.int32, (V_DIM, V_DIM), 1)
        map_of_dim = jnp.where(dim >= HEAD_DIM, 1, 0)
        select = jnp.where(col == map_of_dim, 1.0, 0.0).astype(BF16)

        def chunk_max(c, best):
            kk = k_ref[0, pl.ds(pl.multiple_of(c * KNORM_ROWS, KNORM_ROWS), KNORM_ROWS), :]
            norms = jnp.dot(kk * kk, select, preferred_element_type=F32)
            return jnp.maximum(best, jnp.max(norms, axis=0, keepdims=True))

        kmax_ref[...] = lax.fori_loop(0, k_ref.shape[1] // KNORM_ROWS, chunk_max,
                                      jnp.zeros((1, V_DIM), F32))

    qt = qt_ref[0]

    qf = qt.astype(F32)
    ktf = k_ref[0, pl.ds(pl.multiple_of(qi * tq, tq), tq), :].astype(F32).T
    bound = None
    for mi in range(2):
        rows = slice(mi * HEAD_DIM, (mi + 1) * HEAD_DIM)
        q_sq = jnp.max(jnp.sum(qf[rows] * qf[rows], axis=0, keepdims=True), axis=1, keepdims=True)
        self_min = jnp.min(jnp.sum(qf[rows] * ktf[rows], axis=0, keepdims=True), axis=1,
                           keepdims=True)
        k_sq = kmax_ref[:, mi:mi + 1]
        w_map = NORM_SLACK * jnp.sqrt(q_sq * k_sq) - self_min + SKIP_MARGIN * LOG2E
        bound = w_map if bound is None else jnp.maximum(bound, w_map)
    first_query = jnp.full((1, 1), qi * tq, jnp.int32).astype(F32)
    skip_f = jnp.floor((first_query - bound * inv_slope) * (1.0 / tk))
    j0 = jnp.clip(jnp.clip(skip_f, 0.0, float(2 ** 20)).astype(jnp.int32)[0, 0], 0,
                  qi * (tq // tk))

    zeros = jnp.zeros((HEAD_DIM, tq), BF16)
    row = lax.broadcasted_iota(jnp.int32, (V_DIM, tq), 0)
    ones = jnp.where(row < BIAS_TERMS, 1.0, 0.0).astype(BF16)
    w_ref[:, 0:tq] = jnp.concatenate([qt[0:HEAD_DIM], zeros, ones], axis=0)
    w_ref[:, tq:2 * tq] = jnp.concatenate([zeros, qt[HEAD_DIM:V_DIM], ones], axis=0)
    acc_ref[...] = jnp.zeros(acc_ref.shape, F32)
    p_b[...] = jnp.zeros(p_b.shape, BF16)

    def scores(j, buf):
        start = pl.multiple_of(j * tk, tk)
        k_aug = jnp.concatenate([k_ref[0, pl.ds(start, tk), :], bias_ref[0]], axis=1)
        buf[0][...] = jnp.dot(k_aug, w_ref[...], preferred_element_type=F32)

    def softmax(buf, diag, m, token):
        s_buf, p_buf = buf
        m_out, alpha_out = [], []
        for mi in range(2):
            sm = s_buf[:, mi * tq:(mi + 1) * tq]
            if diag is not None:
                key_pos = lax.broadcasted_iota(jnp.int32, (tk, tq), 0) + diag * tk
                qry_pos = lax.broadcasted_iota(jnp.int32, (tk, tq), 1)
                sm = jnp.where(key_pos <= qry_pos, sm, MASK_VALUE)
            sm3 = sm.reshape(tk // SUBLANES, SUBLANES, tq)
            blk_max = jnp.max(sm3, axis=0)
            for rot in (4, 2, 1):
                blk_max = jnp.maximum(blk_max, pltpu.roll(blk_max, rot, 0))
            m_prev = m[mi] - shift
            m_new = jnp.maximum(m_prev, blk_max)
            alpha_out.append(jnp.exp2(m_prev - m_new))
            m_out.append(m_new)
            arg = (sm3 - (m_new + 0.0 * token[mi])[None]).reshape(tk, tq)
            p_buf[mi] = jnp.exp2(arg.astype(BF16))
        return tuple(m_out), tuple(alpha_out)

    def pv_update(j, buf, alpha):
        _, p_buf = buf
        start = pl.multiple_of(j * tk, tk)
        vt = vt_ref[0, :, pl.ds(start, tk)]
        token = []
        for mi in range(2):
            pv = jnp.dot(vt, p_buf[mi], preferred_element_type=F32)
            acc3 = acc_ref[mi].reshape(V_AUG // SUBLANES, SUBLANES, tq)
            acc_ref[mi] = (alpha[mi][None] * acc3).reshape(V_AUG, tq) + pv
            token.append(pv[V_DIM:V_DIM + SUBLANES])
        return tuple(token)

    def step(j, cur, nxt, state, diag=None, prefetch=True):
        m, alpha_prev, token_cur = state
        if prefetch:
            scores(j + 1, nxt)
        token_nxt = pv_update(jnp.maximum(j - 1, 0), nxt, alpha_prev)
        m, alpha = softmax(cur, diag, m, token_cur)
        return m, alpha, token_nxt

    def finish(cur, nxt, state):
        for d in range(n_diag):
            state = step(first_diag + d, cur, nxt, state, diag=d, prefetch=d + 1 < n_diag)
            cur, nxt = nxt, cur
        pv_update(first_diag + n_diag - 1, nxt, state[1])

    n_diag = tq // tk
    first_diag = qi * n_diag
    n_full = first_diag - j0
    scores(j0, buf_a)

    def two_steps(t, state):
        state = step(j0 + 2 * t, buf_a, buf_b, state)
        return step(j0 + 2 * t + 1, buf_b, buf_a, state)

    row_init = lambda v: (jnp.full((SUBLANES, tq), v, F32), jnp.full((SUBLANES, tq), v, F32))
    state = lax.fori_loop(0, n_full // 2, two_steps,
                          (row_init(MASK_VALUE), row_init(1.0), row_init(0.0)))

    @pl.when(n_full % 2 == 1)
    def _():
        finish(buf_b, buf_a, step(first_diag - 1, buf_a, buf_b, state))

    @pl.when(n_full % 2 == 0)
    def _():
        finish(buf_a, buf_b, state)

    lam = (jnp.exp(jnp.sum(lq1_ref[...] * lk1_ref[...], keepdims=True))
           - jnp.exp(jnp.sum(lq2_ref[...] * lk2_ref[...], keepdims=True)) + LAM_INIT)
    o1 = acc_ref[0, 0:V_DIM, :] * (1.0 / acc_ref[0, V_DIM:V_DIM + 1, :])
    o2 = acc_ref[1, 0:V_DIM, :] * (1.0 / acc_ref[1, V_DIM:V_DIM + 1, :])
    ot = o1 - lam * o2
    o = ot.T
    o_ref[0] = (_rms(o, subln_ref[...]) * (1.0 - LAM_INIT)).astype(BF16)


def _alibi_tables():
    slopes = np.asarray([2.0 ** (-8.0 * (h + 1) / ATTN_HEADS) for h in range(ATTN_HEADS)]) * LOG2E
    rest = slopes[:, None] * np.arange(ATTN_TK, dtype=np.float64)[None, :]
    tile = np.zeros((ATTN_HEADS, ATTN_TK, V_DIM), np.float32)
    for term in range(BIAS_TERMS):
        part = rest.astype(jnp.bfloat16)
        tile[:, :, term] = part.astype(np.float32)
        rest = rest - part.astype(np.float64)
    scalars = np.stack([slopes * ATTN_TK, 1.0 / slopes]).astype(np.float32)
    return jnp.asarray(tile, BF16), jnp.asarray(scalars)


def _attn_call(qt, k, vt, lq1, lk1, lq2, lk2, subln):
    b, s, d = k.shape
    tq, tk = ATTN_TQ, ATTN_TK
    bias, shifts = _alibi_tables()
    vec = lambda n: pl.BlockSpec((1, n), lambda bi, h, i: (0, 0))
    return pl.pallas_call(
        _attn_kernel,
        grid=(b, ATTN_HEADS, s // tq),
        in_specs=[
            pl.BlockSpec(memory_space=pltpu.SMEM),
            pl.BlockSpec((1, V_DIM, tq), lambda bi, h, i: (bi, h, i)),
            pl.BlockSpec((1, s, V_DIM), lambda bi, h, i: (bi, 0, h)),
            pl.BlockSpec((1, V_AUG, s), lambda bi, h, i: (bi, h, 0)),
            pl.BlockSpec((1, tk, V_DIM), lambda bi, h, i: (h, 0, 0)),
            vec(HEAD_DIM), vec(HEAD_DIM), vec(HEAD_DIM), vec(HEAD_DIM),
            vec(V_DIM),
        ],
        out_specs=pl.BlockSpec((1, tq, V_DIM), lambda bi, h, i: (bi, i, h)),
        out_shape=jax.ShapeDtypeStruct((b, s, d), BF16),
        scratch_shapes=[
            pltpu.VMEM((2 * V_DIM, 2 * tq), BF16),
            pltpu.VMEM((2, V_AUG, tq), F32),
            pltpu.VMEM((tk, 2 * tq), F32),
            pltpu.VMEM((tk, 2 * tq), F32),
            pltpu.VMEM((2, tk, tq), BF16),
            pltpu.VMEM((2, tk, tq), BF16),
            pltpu.VMEM((1, V_DIM), F32),
        ],
        compiler_params=pltpu.CompilerParams(
            dimension_semantics=("parallel", "parallel", "arbitrary"),
            vmem_limit_bytes=VMEM_LIMIT_BYTES),
        name="diff_attn",
    )(shifts, qt, k, vt, bias, lq1, lk1, lq2, lk2, subln)


def _ffn_kernel(*refs, tiles_per_seq, with_mixer_proj):
    if with_mixer_proj:
        mix_ref, wproj_ref, gmix_ref, *refs = refs
    (x_ref, gpre_ref, wa_ref, wg_ref, cw_ref, wd_ref, gpost_ref, o_ref,
     hn_ref, abuf_ref, carry_ref, act_ref) = refs
    rows = x_ref.shape[0]
    seq_start = (pl.program_id(0) % tiles_per_seq) == 0
    x = x_ref[...]
    if with_mixer_proj:
        proj = jnp.dot(mix_ref[...], wproj_ref[...], preferred_element_type=F32)
        x = x + _rms(proj, gmix_ref[...])
    hn_ref[...] = _rms(x, gpre_ref[...]).astype(BF16)
    for c in range(FFN_DIM // FFN_CHUNK):
        cols = slice(c * FFN_CHUNK, (c + 1) * FFN_CHUNK)
        a = jnp.dot(hn_ref[...], wa_ref[:, cols], preferred_element_type=F32)
        abuf_ref[0:SUBLANES, :] = jnp.where(seq_start, 0.0, carry_ref[c])
        abuf_ref[SUBLANES:SUBLANES + rows, :] = a
        carry_ref[c] = a[rows - SUBLANES:rows, :]
        cw = cw_ref[:, cols]
        a_conv = (cw[3:4] + abuf_ref[SUBLANES - 2:SUBLANES - 2 + rows, :] * cw[0:1]
                  + abuf_ref[SUBLANES - 1:SUBLANES - 1 + rows, :] * cw[1:2] + a * cw[2:3])
        g = jnp.dot(hn_ref[...], wg_ref[:, cols], preferred_element_type=F32)
        act_ref[:, cols] = (_gelu_tanh(a_conv) * g).astype(BF16)
    f = jnp.dot(act_ref[...], wd_ref[...], preferred_element_type=F32)
    o_ref[...] = x + _rms(f, gpost_ref[...])


def _ffn_call(x2d, seq_len, gpre, w_up, conv_tab, w_down, gpost, mixer_proj=None):
    t, d = x2d.shape
    rows = ROW_TILE
    const = lambda i: (0, 0)
    resident = pl.Buffered(1)
    mixer_specs = [] if mixer_proj is None else [
        pl.BlockSpec((rows, d), lambda i: (i, 0)),
        pl.BlockSpec((d, d), const, pipeline_mode=resident),
        pl.BlockSpec((1, d), const),
    ]
    return pl.pallas_call(
        functools.partial(_ffn_kernel, tiles_per_seq=seq_len // rows,
                          with_mixer_proj=mixer_proj is not None),
        grid=(t // rows,),
        in_specs=mixer_specs + [
            pl.BlockSpec((rows, d), lambda i: (i, 0)),
            pl.BlockSpec((1, d), const),
            pl.BlockSpec((d, FFN_DIM), lambda i: (0, 0), pipeline_mode=resident),
            pl.BlockSpec((d, FFN_DIM), lambda i: (0, 1), pipeline_mode=resident),
            pl.BlockSpec((SUBLANES, FFN_DIM), const),
            pl.BlockSpec((FFN_DIM, d), const, pipeline_mode=resident),
            pl.BlockSpec((1, d), const),
        ],
        out_specs=pl.BlockSpec((rows, d), lambda i: (i, 0)),
        out_shape=jax.ShapeDtypeStruct((t, d), F32),
        scratch_shapes=[
            pltpu.VMEM((rows, d), BF16),
            pltpu.VMEM((rows + SUBLANES, FFN_CHUNK), F32),
            pltpu.VMEM((FFN_DIM // FFN_CHUNK, SUBLANES, FFN_CHUNK), F32),
            pltpu.VMEM((rows, FFN_DIM), BF16),
        ],
        compiler_params=pltpu.CompilerParams(
            dimension_semantics=("arbitrary",),
            vmem_limit_bytes=VMEM_LIMIT_BYTES),
        name="conv_ffn",
    )(*(mixer_proj or ()), x2d, gpre, w_up, w_up, conv_tab, w_down, gpost)


def _sgu_kernel(x_ref, gpre_ref, win_ref, lng_ref, lnb_ref, ws_ref, bst_ref, wout_ref,
                gpost_ref, o_ref, hn_ref, v_ref, vn_ref, y_ref):
    rows = x_ref.shape[0]
    gd = SGU_GROUP_DIM
    x = x_ref[...]
    hn_ref[...] = _rms(x, gpre_ref[...]).astype(BF16)
    for g in range(SGU_GROUPS):
        cols = slice(SGU_WIDTH + g * gd, SGU_WIDTH + (g + 1) * gd)
        v_ref[:, g * gd:(g + 1) * gd] = _gelu_tanh(
            jnp.dot(hn_ref[...], win_ref[:, cols], preferred_element_type=F32))
    v = v_ref[...]
    mu = jnp.mean(v, axis=-1, keepdims=True)
    vc = v - mu
    var = jnp.mean(vc * vc, axis=-1, keepdims=True)
    vn_ref[...] = (vc * lax.rsqrt(var + LN_EPS) * lng_ref[...] + lnb_ref[...]).astype(BF16)
    t_pos = lax.broadcasted_iota(jnp.int32, (CHUNK, CHUNK), 0)
    s_pos = lax.broadcasted_iota(jnp.int32, (CHUNK, CHUNK), 1)
    for g in range(SGU_GROUPS):
        cols = slice(g * gd, (g + 1) * gd)
        u = _gelu_tanh(jnp.dot(hn_ref[...], win_ref[:, cols], preferred_element_type=F32))
        w = jnp.where(s_pos <= t_pos, ws_ref[g], 0.0).astype(BF16)
        bias = bst_ref[:, g:g + 1]
        for c in range(rows // CHUNK):
            r = slice(c * CHUNK, (c + 1) * CHUNK)
            s = jnp.dot(w, vn_ref[r, cols], preferred_element_type=F32) + bias
            y_ref[r, cols] = (u[r] * s).astype(BF16)
    out = jnp.dot(y_ref[...], wout_ref[...], preferred_element_type=F32)
    o_ref[...] = x + _rms(out, gpost_ref[...])


def _sgu_call(x2d, gpre, w_in, ln_g, ln_b, w_s, b_st, w_out, gpost):
    t, d = x2d.shape
    rows = ROW_TILE
    const = lambda i: (0, 0)
    resident = pl.Buffered(1)
    return pl.pallas_call(
        _sgu_kernel,
        grid=(t // rows,),
        in_specs=[
            pl.BlockSpec((rows, d), lambda i: (i, 0)),
            pl.BlockSpec((1, d), const),
            pl.BlockSpec((d, 2 * SGU_WIDTH), const, pipeline_mode=resident),
            pl.BlockSpec((1, SGU_WIDTH), const),
            pl.BlockSpec((1, SGU_WIDTH), const),
            pl.BlockSpec((SGU_GROUPS, CHUNK, CHUNK), lambda i: (0, 0, 0)),
            pl.BlockSpec((CHUNK, SGU_GROUPS), const),
            pl.BlockSpec((SGU_WIDTH, d), const, pipeline_mode=resident),
            pl.BlockSpec((1, d), const),
        ],
        out_specs=pl.BlockSpec((rows, d), lambda i: (i, 0)),
        out_shape=jax.ShapeDtypeStruct((t, d), F32),
        scratch_shapes=[
            pltpu.VMEM((rows, d), BF16),
            pltpu.VMEM((rows, SGU_WIDTH), F32),
            pltpu.VMEM((rows, SGU_WIDTH), BF16),
            pltpu.VMEM((rows, SGU_WIDTH), BF16),
        ],
        compiler_params=pltpu.CompilerParams(
            dimension_semantics=("parallel",),
            vmem_limit_bytes=VMEM_LIMIT_BYTES),
        name="chunked_sgu",
    )(x2d, gpre, w_in, ln_g, ln_b, w_s, b_st, w_out, gpost)


def _conv_table(conv_w, conv_b):
    pad = jnp.zeros((SUBLANES - 4, FFN_DIM), F32)
    return jnp.concatenate([conv_w, conv_b[None, :], pad], axis=0)


def _row(v):
    return v.reshape(1, -1)


def kernel(x, attn_w_qkv, attn_lambda_q1, attn_lambda_k1, attn_lambda_q2, attn_lambda_k2, attn_subln, attn_w_o, sgu_w_in, sgu_ln_g, sgu_ln_b, sgu_w_s, sgu_b_s, sgu_w_out, norm_mix_pre, norm_mix_post, norm_ffn_pre, norm_ffn_post, ffn_w_up, ffn_conv_w, ffn_conv_b, ffn_w_down):
    b, s, d = x.shape
    assert d == D_MODEL and s % max(QKV_ROWS, ROW_TILE, ATTN_TQ) == 0 and ATTN_TQ % ATTN_TK == 0

    w_qkv = attn_w_qkv[0]
    wqt = (w_qkv[:, :d].T * (HEAD_DIM ** -0.5 * LOG2E)).astype(BF16)
    wk = w_qkv[:, d:2 * d].astype(BF16)
    wvt = w_qkv[:, 2 * d:].T.astype(BF16)
    qt, k, vt = _qkv_call(x, _row(norm_mix_pre[0]), wqt, wk, wvt)
    attn = _attn_call(qt, k, vt, _row(attn_lambda_q1[0]), _row(attn_lambda_k1[0]),
                      _row(attn_lambda_q2[0]), _row(attn_lambda_k2[0]), _row(attn_subln[0]))
    x2d = x.reshape(b * s, d)
    x2d = _ffn_call(x2d, s, _row(norm_ffn_pre[0]), ffn_w_up[0].astype(BF16),
                    _conv_table(ffn_conv_w[0], ffn_conv_b[0]), ffn_w_down[0].astype(BF16),
                    _row(norm_ffn_post[0]),
                    mixer_proj=(attn.reshape(b * s, d), attn_w_o[0].astype(BF16),
                                _row(norm_mix_post[0])))

    x2d = _sgu_call(x2d, _row(norm_mix_pre[1]), sgu_w_in[0].astype(BF16), _row(sgu_ln_g[0]),
                    _row(sgu_ln_b[0]), sgu_w_s[0], sgu_b_s[0].T, sgu_w_out[0].astype(BF16),
                    _row(norm_mix_post[1]))
    x2d = _ffn_call(x2d, s, _row(norm_ffn_pre[1]), ffn_w_up[1].astype(BF16),
                    _conv_table(ffn_conv_w[1], ffn_conv_b[1]), ffn_w_down[1].astype(BF16),
                    _row(norm_ffn_post[1]))
    return x2d.reshape(b, s, d)
```

```python
import functools
import math

import jax
import jax.numpy as jnp
import numpy as np
from jax import lax
from jax.experimental import pallas as pl
from jax.experimental.pallas import tpu as pltpu

F32 = jnp.float32
BF16 = jnp.bfloat16

D_MODEL = 1024
ATTN_HEADS = 8
HEAD_DIM = 64
V_DIM = 2 * HEAD_DIM
SGU_WIDTH = 2 * D_MODEL
SGU_GROUPS = 8
SGU_GROUP_DIM = SGU_WIDTH // SGU_GROUPS
CHUNK = 128
FFN_DIM = 2816
NORM_EPS = 1e-6
LN_EPS = 1e-5
LAM_INIT = 0.8 - 0.6 * math.exp(-0.3 * 0)

MXU_DIM = 256
SUBLANES = 8
V_PAD = 16
V_AUG = V_DIM + V_PAD
LOG2E = math.log2(math.e)
BIAS_TERMS = 4
VMEM_LIMIT_BYTES = 56 * 1024 * 1024

QKV_ROWS = 512
ATTN_TQ = 512
ATTN_TK = 512
ROW_TILE = 1024
ROW_TILE_WITH_PROJ = 512
FFN_CHUNK = MXU_DIM
MASK_VALUE = -1e30
KNORM_ROWS = 1024
SKIP_MARGIN = 110.0
NORM_SLACK = 1.02

_NT_DIMS = (((1,), (1,)), ((), ()))


def _rms(x, gain):
    return x * lax.rsqrt(jnp.mean(x * x, axis=-1, keepdims=True) + NORM_EPS) * gain


def _gelu_tanh(x):
    return x * (0.5 * (1.0 + jnp.tanh(0.7978845608028654 * (x + 0.044715 * (x * x * x)))))


def _qkv_kernel(x_ref, g_ref, wqt_ref, wk_ref, wvt_ref, qt_ref, k_ref, vt_ref):
    rows = x_ref.shape[1]
    hn = _rms(x_ref[0], g_ref[...]).astype(BF16)
    k_ref[0] = jnp.dot(hn, wk_ref[...], preferred_element_type=F32).astype(BF16)
    qt_ref[0] = lax.dot_general(wqt_ref[...], hn, _NT_DIMS,
                                preferred_element_type=F32).astype(BF16)
    vt = lax.dot_general(wvt_ref[...], hn, _NT_DIMS,
                         preferred_element_type=F32).astype(BF16)
    pad_row = lax.broadcasted_iota(jnp.int32, (V_PAD, rows), 0)
    ones_rows = jnp.where(pad_row == 0, 1.0, 0.0).astype(BF16)
    for h in range(ATTN_HEADS):
        vt_ref[0, h * V_AUG:h * V_AUG + V_DIM, :] = vt[h * V_DIM:(h + 1) * V_DIM]
        vt_ref[0, h * V_AUG + V_DIM:(h + 1) * V_AUG, :] = ones_rows


def _qkv_call(x, gain, wqt, wk, wvt):
    b, s, d = x.shape
    rows = QKV_ROWS
    const = lambda bi, i: (0, 0)
    return pl.pallas_call(
        _qkv_kernel,
        grid=(b, s // rows),
        in_specs=[
            pl.BlockSpec((1, rows, d), lambda bi, i: (bi, i, 0)),
            pl.BlockSpec((1, d), const),
            pl.BlockSpec((d, d), const),
            pl.BlockSpec((d, d), const),
            pl.BlockSpec((d, d), const),
        ],
        out_specs=[
            pl.BlockSpec((1, d, rows), lambda bi, i: (bi, 0, i)),
            pl.BlockSpec((1, rows, d), lambda bi, i: (bi, i, 0)),
            pl.BlockSpec((1, ATTN_HEADS * V_AUG, rows), lambda bi, i: (bi, 0, i)),
        ],
        out_shape=[
            jax.ShapeDtypeStruct((b, d, s), BF16),
            jax.ShapeDtypeStruct((b, s, d), BF16),
            jax.ShapeDtypeStruct((b, ATTN_HEADS * V_AUG, s), BF16),
        ],
        compiler_params=pltpu.CompilerParams(
            dimension_semantics=("parallel", "parallel"),
            vmem_limit_bytes=VMEM_LIMIT_BYTES),
        name="qkv_proj",
    )(x, gain, wqt, wk, wvt)


def _attn_kernel(shift_ref, qt_ref, k_ref, vt_ref, bias_ref, lq1_ref, lk1_ref, lq2_ref,
                 lk2_ref, subln_ref, o_ref, w_ref, acc_ref, s_a, s_b, p_a, p_b, kmax_ref):
    tq, tk = ATTN_TQ, ATTN_TK
    buf_a = (s_a, p_a)
    buf_b = (s_b, p_b)
    head = pl.program_id(1)
    qi = pl.program_id(2)
    shift = shift_ref[0, head]
    inv_slope = shift_ref[1, head]

    @pl.when(qi == 0)
    def _():
        dim = lax.broadcasted_iota(jnp.int32, (V_DIM, V_DIM), 0)
        col = lax.broadcasted_iota(jnp.int32, (V_DIM, V_DIM), 1)
        map_of_dim = jnp.where(dim >= HEAD_DIM, 1, 0)
        select = jnp.where(col == map_of_dim, 1.0, 0.0).astype(BF16)

        def chunk_max(c, best):
            kk = k_ref[0, pl.ds(pl.multiple_of(c * KNORM_ROWS, KNORM_ROWS), KNORM_ROWS), :]
            norms = jnp.dot(kk * kk, select, preferred_element_type=F32)
            return jnp.maximum(best, jnp.max(norms, axis=0, keepdims=True))

        kmax_ref[...] = lax.fori_loop(0, k_ref.shape[1] // KNORM_ROWS, chunk_max,
                                      jnp.zeros((1, V_DIM), F32))

    qt = qt_ref[0]

    qf = qt.astype(F32)
    ktf = k_ref[0, pl.ds(pl.multiple_of(qi * tq, tq), tq), :].astype(F32).T
    bound = None
    for mi in range(2):
        rows = slice(mi * HEAD_DIM, (mi + 1) * HEAD_DIM)
        q_sq = jnp.max(jnp.sum(qf[rows] * qf[rows], axis=0, keepdims=True), axis=1, keepdims=True)
        self_min = jnp.min(jnp.sum(qf[rows] * ktf[rows], axis=0, keepdims=True), axis=1,
                           keepdims=True)
        k_sq = kmax_ref[:, mi:mi + 1]
        w_map = NORM_SLACK * jnp.sqrt(q_sq * k_sq) - self_min + SKIP_MARGIN * LOG2E
        bound = w_map if bound is None else jnp.maximum(bound, w_map)
    first_query = jnp.full((1, 1), qi * tq, jnp.int32).astype(F32)
    skip_f = jnp.floor((first_query - bound * inv_slope) * (1.0 / tk))
    j0 = jnp.clip(jnp.clip(skip_f, 0.0, float(2 ** 20)).astype(jnp.int32)[0, 0], 0,
                  qi * (tq // tk))

    zeros = jnp.zeros((HEAD_DIM, tq), BF16)
    row = lax.broadcasted_iota(jnp.int32, (V_DIM, tq), 0)
    ones = jnp.where(row < BIAS_TERMS, 1.0, 0.0).astype(BF16)
    w_ref[:, 0:tq] = jnp.concatenate([qt[0:HEAD_DIM], zeros, ones], axis=0)
    w_ref[:, tq:2 * tq] = jnp.concatenate([zeros, qt[HEAD_DIM:V_DIM], ones], axis=0)
    acc_ref[...] = jnp.zeros(acc_ref.shape, F32)
    p_b[...] = jnp.zeros(p_b.shape, BF16)

    def scores(j, buf):
        start = pl.multiple_of(j * tk, tk)
        k_aug = jnp.concatenate([k_ref[0, pl.ds(start, tk), :], bias_ref[0]], axis=1)
        buf[0][...] = jnp.dot(k_aug, w_ref[...], preferred_element_type=F32)

    def softmax(buf, diag, m, token):
        s_buf, p_buf = buf
        m_out, alpha_out = [], []
        for mi in range(2):
            sm = s_buf[:, mi * tq:(mi + 1) * tq]
            if diag is not None:
                key_pos = lax.broadcasted_iota(jnp.int32, (tk, tq), 0) + diag * tk
                qry_pos = lax.broadcasted_iota(jnp.int32, (tk, tq), 1)
                sm = jnp.where(key_pos <= qry_pos, sm, MASK_VALUE)
            sm3 = sm.reshape(tk // SUBLANES, SUBLANES, tq)
            blk_max = jnp.max(sm3, axis=0)
            for rot in (4, 2, 1):
                blk_max = jnp.maximum(blk_max, pltpu.roll(blk_max, rot, 0))
            m_prev = m[mi] - shift
            m_new = jnp.maximum(m_prev, blk_max)
            alpha_out.append(jnp.exp2(m_prev - m_new))
            m_out.append(m_new)
            arg = (sm3 - (m_new + 0.0 * token[mi])[None]).reshape(tk, tq)
            p_buf[mi] = jnp.exp2(arg.astype(BF16))
        return tuple(m_out), tuple(alpha_out)

    def pv_update(j, buf, alpha):
        _, p_buf = buf
        start = pl.multiple_of(j * tk, tk)
        vt = vt_ref[0, :, pl.ds(start, tk)]
        token = []
        for mi in range(2):
            pv = jnp.dot(vt, p_buf[mi], preferred_element_type=F32)
            acc3 = acc_ref[mi].reshape(V_AUG // SUBLANES, SUBLANES, tq)
            acc_ref[mi] = (alpha[mi][None] * acc3).reshape(V_AUG, tq) + pv
            token.append(pv[V_DIM:V_DIM + SUBLANES])
        return tuple(token)

    def step(j, cur, nxt, state, diag=None, prefetch=True):
        m, alpha_prev, token_cur = state
        if prefetch:
            scores(j + 1, nxt)
        token_nxt = pv_update(jnp.maximum(j - 1, 0), nxt, alpha_prev)
        m, alpha = softmax(cur, diag, m, token_cur)
        return m, alpha, token_nxt

    def finish(cur, nxt, state):
        for d in range(n_diag):
            state = step(first_diag + d, cur, nxt, state, diag=d, prefetch=d + 1 < n_diag)
            cur, nxt = nxt, cur
        pv_update(first_diag + n_diag - 1, nxt, state[1])

    n_diag = tq // tk
    first_diag = qi * n_diag
    n_full = first_diag - j0
    scores(j0, buf_a)

    def two_steps(t, state):
        state = step(j0 + 2 * t, buf_a, buf_b, state)
        return step(j0 + 2 * t + 1, buf_b, buf_a, state)

    row_init = lambda v: (jnp.full((SUBLANES, tq), v, F32), jnp.full((SUBLANES, tq), v, F32))
    state = lax.fori_loop(0, n_full // 2, two_steps,
                          (row_init(MASK_VALUE), row_init(1.0), row_init(0.0)))

    @pl.when(n_full % 2 == 1)
    def _():
        finish(buf_b, buf_a, step(first_diag - 1, buf_a, buf_b, state))

    @pl.when(n_full % 2 == 0)
    def _():
        finish(buf_a, buf_b, state)

    lam = (jnp.exp(jnp.sum(lq1_ref[...] * lk1_ref[...], keepdims=True))
           - jnp.exp(jnp.sum(lq2_ref[...] * lk2_ref[...], keepdims=True)) + LAM_INIT)
    o1 = acc_ref[0, 0:V_DIM, :] * (1.0 / acc_ref[0, V_DIM:V_DIM + 1, :])
    o2 = acc_ref[1, 0:V_DIM, :] * (1.0 / acc_ref[1, V_DIM:V_DIM + 1, :])
    ot = o1 - lam * o2
    inv_rms = lax.rsqrt(jnp.mean(ot * ot, axis=0, keepdims=True) + NORM_EPS)
    gain = jnp.tile(subln_ref[...], (1, tq // V_DIM))
    o_ref[0] = (ot * inv_rms * gain * (1.0 - LAM_INIT)).T.astype(BF16)


def _alibi_tables():
    slopes = np.asarray([2.0 ** (-8.0 * (h + 1) / ATTN_HEADS) for h in range(ATTN_HEADS)]) * LOG2E
    rest = slopes[:, None] * np.arange(ATTN_TK, dtype=np.float64)[None, :]
    tile = np.zeros((ATTN_HEADS, ATTN_TK, V_DIM), np.float32)
    for term in range(BIAS_TERMS):
        part = rest.astype(jnp.bfloat16)
        tile[:, :, term] = part.astype(np.float32)
        rest = rest - part.astype(np.float64)
    scalars = np.stack([slopes * ATTN_TK, 1.0 / slopes]).astype(np.float32)
    return jnp.asarray(tile, BF16), jnp.asarray(scalars)


def _attn_call(qt, k, vt, lq1, lk1, lq2, lk2, subln):
    b, s, d = k.shape
    tq, tk = ATTN_TQ, ATTN_TK
    bias, shifts = _alibi_tables()
    vec = lambda n: pl.BlockSpec((1, n), lambda bi, h, i: (0, 0))
    return pl.pallas_call(
        _attn_kernel,
        grid=(b, ATTN_HEADS, s // tq),
        in_specs=[
            pl.BlockSpec(memory_space=pltpu.SMEM),
            pl.BlockSpec((1, V_DIM, tq), lambda bi, h, i: (bi, h, i)),
            pl.BlockSpec((1, s, V_DIM), lambda bi, h, i: (bi, 0, h)),
            pl.BlockSpec((1, V_AUG, s), lambda bi, h, i: (bi, h, 0)),
            pl.BlockSpec((1, tk, V_DIM), lambda bi, h, i: (h, 0, 0)),
            vec(HEAD_DIM), vec(HEAD_DIM), vec(HEAD_DIM), vec(HEAD_DIM),
            pl.BlockSpec((V_DIM, V_DIM), lambda bi, h, i: (0, 0)),
        ],
        out_specs=pl.BlockSpec((1, tq, V_DIM), lambda bi, h, i: (bi, i, h)),
        out_shape=jax.ShapeDtypeStruct((b, s, d), BF16),
        scratch_shapes=[
            pltpu.VMEM((2 * V_DIM, 2 * tq), BF16),
            pltpu.VMEM((2, V_AUG, tq), F32),
            pltpu.VMEM((tk, 2 * tq), F32),
            pltpu.VMEM((tk, 2 * tq), F32),
            pltpu.VMEM((2, tk, tq), BF16),
            pltpu.VMEM((2, tk, tq), BF16),
            pltpu.VMEM((1, V_DIM), F32),
        ],
        compiler_params=pltpu.CompilerParams(
            dimension_semantics=("parallel", "parallel", "arbitrary"),
            vmem_limit_bytes=VMEM_LIMIT_BYTES),
        name="diff_attn",
    )(shifts, qt, k, vt, bias, lq1, lk1, lq2, lk2, subln)


def _ffn_kernel(*refs, tiles_per_seq, with_mixer_proj):
    if with_mixer_proj:
        mix_ref, wproj_ref, gmix_ref, *refs = refs
    (x_ref, gpre_ref, wa_ref, wg_ref, cw_ref, wd_ref, gpost_ref, o_ref,
     hn_ref, abuf_ref, carry_ref, act_ref) = refs
    rows = x_ref.shape[0]
    seq_start = (pl.program_id(0) % tiles_per_seq) == 0
    x = x_ref[...]
    if with_mixer_proj:
        proj = jnp.dot(mix_ref[...], wproj_ref[...], preferred_element_type=F32)
        x = x + _rms(proj, gmix_ref[...])
    hn_ref[...] = _rms(x, gpre_ref[...]).astype(BF16)
    for c in range(FFN_DIM // FFN_CHUNK):
        cols = slice(c * FFN_CHUNK, (c + 1) * FFN_CHUNK)
        a = jnp.dot(hn_ref[...], wa_ref[:, cols], preferred_element_type=F32)
        abuf_ref[0:SUBLANES, :] = jnp.where(seq_start, 0.0, carry_ref[c])
        abuf_ref[SUBLANES:SUBLANES + rows, :] = a
        carry_ref[c] = a[rows - SUBLANES:rows, :]
        cw = cw_ref[:, cols]
        a_conv = (cw[3:4] + abuf_ref[SUBLANES - 2:SUBLANES - 2 + rows, :] * cw[0:1]
                  + abuf_ref[SUBLANES - 1:SUBLANES - 1 + rows, :] * cw[1:2] + a * cw[2:3])
        g = jnp.dot(hn_ref[...], wg_ref[:, cols], preferred_element_type=F32)
        act_ref[:, cols] = (_gelu_tanh(a_conv) * g).astype(BF16)
    f = jnp.dot(act_ref[...], wd_ref[...], preferred_element_type=F32)
    o_ref[...] = x + _rms(f, gpost_ref[...])


def _ffn_call(x2d, seq_len, gpre, w_up, conv_tab, w_down, gpost, mixer_proj=None):
    t, d = x2d.shape
    rows = ROW_TILE if mixer_proj is None else ROW_TILE_WITH_PROJ
    const = lambda i: (0, 0)
    resident = pl.Buffered(1)
    mixer_specs = [] if mixer_proj is None else [
        pl.BlockSpec((rows, d), lambda i: (i, 0)),
        pl.BlockSpec((d, d), const, pipeline_mode=resident),
        pl.BlockSpec((1, d), const),
    ]
    return pl.pallas_call(
        functools.partial(_ffn_kernel, tiles_per_seq=seq_len // rows,
                          with_mixer_proj=mixer_proj is not None),
        grid=(t // rows,),
        in_specs=mixer_specs + [
            pl.BlockSpec((rows, d), lambda i: (i, 0)),
            pl.BlockSpec((1, d), const),
            pl.BlockSpec((d, FFN_DIM), lambda i: (0, 0), pipeline_mode=resident),
            pl.BlockSpec((d, FFN_DIM), lambda i: (0, 1), pipeline_mode=resident),
            pl.BlockSpec((SUBLANES, FFN_DIM), const),
            pl.BlockSpec((FFN_DIM, d), const, pipeline_mode=resident),
            pl.BlockSpec((1, d), const),
        ],
        out_specs=pl.BlockSpec((rows, d), lambda i: (i, 0)),
        out_shape=jax.ShapeDtypeStruct((t, d), F32),
        scratch_shapes=[
            pltpu.VMEM((rows, d), BF16),
            pltpu.VMEM((rows + SUBLANES, FFN_CHUNK), F32),
            pltpu.VMEM((FFN_DIM // FFN_CHUNK, SUBLANES, FFN_CHUNK), F32),
            pltpu.VMEM((rows, FFN_DIM), BF16),
        ],
        compiler_params=pltpu.CompilerParams(
            dimension_semantics=("arbitrary",),
            vmem_limit_bytes=VMEM_LIMIT_BYTES),
        name="conv_ffn",
    )(*(mixer_proj or ()), x2d, gpre, w_up, w_up, conv_tab, w_down, gpost)


def _sgu_kernel(x_ref, gpre_ref, win_ref, lng_ref, lnb_ref, ws_ref, bst_ref, wout_ref,
                gpost_ref, o_ref, hn_ref, v_ref, vn_ref, y_ref):
    rows = x_ref.shape[0]
    gd = SGU_GROUP_DIM
    x = x_ref[...]
    hn_ref[...] = _rms(x, gpre_ref[...]).astype(BF16)
    for g in range(SGU_GROUPS):
        cols = slice(SGU_WIDTH + g * gd, SGU_WIDTH + (g + 1) * gd)
        v_ref[:, g * gd:(g + 1) * gd] = _gelu_tanh(
            jnp.dot(hn_ref[...], win_ref[:, cols], preferred_element_type=F32))
    v = v_ref[...]
    mu = jnp.mean(v, axis=-1, keepdims=True)
    vc = v - mu
    var = jnp.mean(vc * vc, axis=-1, keepdims=True)
    vn_ref[...] = (vc * lax.rsqrt(var + LN_EPS) * lng_ref[...] + lnb_ref[...]).astype(BF16)
    t_pos = lax.broadcasted_iota(jnp.int32, (CHUNK, CHUNK), 0)
    s_pos = lax.broadcasted_iota(jnp.int32, (CHUNK, CHUNK), 1)
    for g in range(SGU_GROUPS):
        cols = slice(g * gd, (g + 1) * gd)
        u = _gelu_tanh(jnp.dot(hn_ref[...], win_ref[:, cols], preferred_element_type=F32))
        w = jnp.where(s_pos <= t_pos, ws_ref[g], 0.0).astype(BF16)
        bias = bst_ref[:, g:g + 1]
        for c in range(rows // CHUNK):
            r = slice(c * CHUNK, (c + 1) * CHUNK)
            s = jnp.dot(w, vn_ref[r, cols], preferred_element_type=F32) + bias
            y_ref[r, cols] = (u[r] * s).astype(BF16)
    out = jnp.dot(y_ref[...], wout_ref[...], preferred_element_type=F32)
    o_ref[...] = x + _rms(out, gpost_ref[...])


def _sgu_call(x2d, gpre, w_in, ln_g, ln_b, w_s, b_st, w_out, gpost):
    t, d = x2d.shape
    rows = ROW_TILE
    const = lambda i: (0, 0)
    resident = pl.Buffered(1)
    return pl.pallas_call(
        _sgu_kernel,
        grid=(t // rows,),
        in_specs=[
            pl.BlockSpec((rows, d), lambda i: (i, 0)),
            pl.BlockSpec((1, d), const),
            pl.BlockSpec((d, 2 * SGU_WIDTH), const, pipeline_mode=resident),
            pl.BlockSpec((1, SGU_WIDTH), const),
            pl.BlockSpec((1, SGU_WIDTH), const),
            pl.BlockSpec((SGU_GROUPS, CHUNK, CHUNK), lambda i: (0, 0, 0)),
            pl.BlockSpec((CHUNK, SGU_GROUPS), const),
            pl.BlockSpec((SGU_WIDTH, d), const, pipeline_mode=resident),
            pl.BlockSpec((1, d), const),
        ],
        out_specs=pl.BlockSpec((rows, d), lambda i: (i, 0)),
        out_shape=jax.ShapeDtypeStruct((t, d), F32),
        scratch_shapes=[
            pltpu.VMEM((rows, d), BF16),
            pltpu.VMEM((rows, SGU_WIDTH), F32),
            pltpu.VMEM((rows, SGU_WIDTH), BF16),
            pltpu.VMEM((rows, SGU_WIDTH), BF16),
        ],
        compiler_params=pltpu.CompilerParams(
            dimension_semantics=("parallel",),
            vmem_limit_bytes=VMEM_LIMIT_BYTES),
        name="chunked_sgu",
    )(x2d, gpre, w_in, ln_g, ln_b, w_s, b_st, w_out, gpost)


def _conv_table(conv_w, conv_b):
    pad = jnp.zeros((SUBLANES - 4, FFN_DIM), F32)
    return jnp.concatenate([conv_w, conv_b[None, :], pad], axis=0)


def _row(v):
    return v.reshape(1, -1)


def kernel(x, attn_w_qkv, attn_lambda_q1, attn_lambda_k1, attn_lambda_q2, attn_lambda_k2, attn_subln, attn_w_o, sgu_w_in, sgu_ln_g, sgu_ln_b, sgu_w_s, sgu_b_s, sgu_w_out, norm_mix_pre, norm_mix_post, norm_ffn_pre, norm_ffn_post, ffn_w_up, ffn_conv_w, ffn_conv_b, ffn_w_down):
    b, s, d = x.shape
    assert d == D_MODEL and s % max(QKV_ROWS, ROW_TILE, ATTN_TQ) == 0 and ATTN_TQ % ATTN_TK == 0

    w_qkv = attn_w_qkv[0]
    wqt = (w_qkv[:, :d].T * (HEAD_DIM ** -0.5 * LOG2E)).astype(BF16)
    wk = w_qkv[:, d:2 * d].astype(BF16)
    wvt = w_qkv[:, 2 * d:].T.astype(BF16)
    qt, k, vt = _qkv_call(x, _row(norm_mix_pre[0]), wqt, wk, wvt)
    attn = _attn_call(qt, k, vt, _row(attn_lambda_q1[0]), _row(attn_lambda_k1[0]),
                      _row(attn_lambda_q2[0]), _row(attn_lambda_k2[0]),
                      jnp.broadcast_to(attn_subln[0][:, None], (V_DIM, V_DIM)))
    x2d = x.reshape(b * s, d)
    x2d = _ffn_call(x2d, s, _row(norm_ffn_pre[0]), ffn_w_up[0].astype(BF16),
                    _conv_table(ffn_conv_w[0], ffn_conv_b[0]), ffn_w_down[0].astype(BF16),
                    _row(norm_ffn_post[0]),
                    mixer_proj=(attn.reshape(b * s, d), attn_w_o[0].astype(BF16),
                                _row(norm_mix_post[0])))

    x2d = _sgu_call(x2d, _row(norm_mix_pre[1]), sgu_w_in[0].astype(BF16), _row(sgu_ln_g[0]),
                    _row(sgu_ln_b[0]), sgu_w_s[0], sgu_b_s[0].T, sgu_w_out[0].astype(BF16),
                    _row(norm_mix_post[1]))
    x2d = _ffn_call(x2d, s, _row(norm_ffn_pre[1]), ffn_w_up[1].astype(BF16),
                    _conv_table(ffn_conv_w[1], ffn_conv_b[1]), ffn_w_down[1].astype(BF16),
                    _row(norm_ffn_post[1]))
    return x2d.reshape(b, s, d)
```

```python
import functools
import math

import jax
import jax.numpy as jnp
import numpy as np
from jax import lax
from jax.experimental import pallas as pl
from jax.experimental.pallas import tpu as pltpu

F32 = jnp.float32
BF16 = jnp.bfloat16

D_MODEL = 1024
ATTN_HEADS = 8
HEAD_DIM = 64
V_DIM = 2 * HEAD_DIM
SGU_WIDTH = 2 * D_MODEL
SGU_GROUPS = 8
SGU_GROUP_DIM = SGU_WIDTH // SGU_GROUPS
CHUNK = 128
FFN_DIM = 2816
NORM_EPS = 1e-6
LN_EPS = 1e-5
LAM_INIT = 0.8 - 0.6 * math.exp(-0.3 * 0)

MXU_DIM = 256
SUBLANES = 8
V_PAD = 16
V_AUG = V_DIM + V_PAD
LOG2E = math.log2(math.e)
BIAS_TERMS = 4
VMEM_LIMIT_BYTES = 56 * 1024 * 1024

QKV_ROWS = 512
ATTN_TQ = 512
ATTN_TK = 512
ROW_TILE = 1024
ROW_TILE_WITH_PROJ = 512
FFN_CHUNK = MXU_DIM
MASK_VALUE = -1e30
KNORM_ROWS = 1024
SKIP_MARGIN = 110.0
NORM_SLACK = 1.02

_NT_DIMS = (((1,), (1,)), ((), ()))


def _rms(x, gain):
    return x * lax.rsqrt(jnp.mean(x * x, axis=-1, keepdims=True) + NORM_EPS) * gain


def _gelu_tanh(x):
    return x * (0.5 * (1.0 + jnp.tanh(0.7978845608028654 * (x + 0.044715 * (x * x * x)))))


def _qkv_kernel(x_ref, g_ref, wqt_ref, wk_ref, wvt_ref, qt_ref, k_ref, vt_ref):
    rows = x_ref.shape[1]
    hn = _rms(x_ref[0], g_ref[...]).astype(BF16)
    k_ref[0] = jnp.dot(hn, wk_ref[...], preferred_element_type=F32).astype(BF16)
    qt_ref[0] = lax.dot_general(wqt_ref[...], hn, _NT_DIMS,
                                preferred_element_type=F32).astype(BF16)
    vt = lax.dot_general(wvt_ref[...], hn, _NT_DIMS,
                         preferred_element_type=F32).astype(BF16)
    pad_row = lax.broadcasted_iota(jnp.int32, (V_PAD, rows), 0)
    ones_rows = jnp.where(pad_row == 0, 1.0, 0.0).astype(BF16)
    for h in range(ATTN_HEADS):
        vt_ref[0, h * V_AUG:h * V_AUG + V_DIM, :] = vt[h * V_DIM:(h + 1) * V_DIM]
        vt_ref[0, h * V_AUG + V_DIM:(h + 1) * V_AUG, :] = ones_rows


def _qkv_call(x, gain, wqt, wk, wvt):
    b, s, d = x.shape
    rows = QKV_ROWS
    const = lambda bi, i: (0, 0)
    return pl.pallas_call(
        _qkv_kernel,
        grid=(b, s // rows),
        in_specs=[
            pl.BlockSpec((1, rows, d), lambda bi, i: (bi, i, 0)),
            pl.BlockSpec((1, d), const),
            pl.BlockSpec((d, d), const),
            pl.BlockSpec((d, d), const),
            pl.BlockSpec((d, d), const),
        ],
        out_specs=[
            pl.BlockSpec((1, d, rows), lambda bi, i: (bi, 0, i)),
            pl.BlockSpec((1, rows, d), lambda bi, i: (bi, i, 0)),
            pl.BlockSpec((1, ATTN_HEADS * V_AUG, rows), lambda bi, i: (bi, 0, i)),
        ],
        out_shape=[
            jax.ShapeDtypeStruct((b, d, s), BF16),
            jax.ShapeDtypeStruct((b, s, d), BF16),
            jax.ShapeDtypeStruct((b, ATTN_HEADS * V_AUG, s), BF16),
        ],
        compiler_params=pltpu.CompilerParams(
            dimension_semantics=("parallel", "parallel"),
            vmem_limit_bytes=VMEM_LIMIT_BYTES),
        name="qkv_proj",
    )(x, gain, wqt, wk, wvt)


def _attn_kernel(shift_ref, qt_ref, qt_next_ref, k_ref, vt_ref, bias_ref, lq1_ref, lk1_ref,
                 lq2_ref, lk2_ref, subln_ref, o_ref, w_ref, w_next_ref, acc_ref, s_a, s_b,
                 p_a, p_b, kmax_ref, first_block_ref):
    tq, tk = ATTN_TQ, ATTN_TK
    buf_a = (s_a, p_a)
    buf_b = (s_b, p_b)
    head = pl.program_id(1)
    qi = pl.program_id(2)
    shift = shift_ref[0, head]
    inv_slope = shift_ref[1, head]

    @pl.when(qi == 0)
    def _():
        dim = lax.broadcasted_iota(jnp.int32, (V_DIM, V_DIM), 0)
        col = lax.broadcasted_iota(jnp.int32, (V_DIM, V_DIM), 1)
        map_of_dim = jnp.where(dim >= HEAD_DIM, 1, 0)
        select = jnp.where(col == map_of_dim, 1.0, 0.0).astype(BF16)

        def chunk_max(c, best):
            kk = k_ref[0, pl.ds(pl.multiple_of(c * KNORM_ROWS, KNORM_ROWS), KNORM_ROWS), :]
            norms = jnp.dot(kk * kk, select, preferred_element_type=F32)
            return jnp.maximum(best, jnp.max(norms, axis=0, keepdims=True))

        kmax_ref[...] = lax.fori_loop(0, k_ref.shape[1] // KNORM_ROWS, chunk_max,
                                      jnp.zeros((1, V_DIM), F32))

    def scores(j, buf, weights=w_ref):
        start = pl.multiple_of(j * tk, tk)
        k_aug = jnp.concatenate([k_ref[0, pl.ds(start, tk), :], bias_ref[0]], axis=1)
        buf[0][...] = jnp.dot(k_aug, weights[...], preferred_element_type=F32)

    def open_tile(tile, qt, weights):
        qf = qt.astype(F32)
        ktf = k_ref[0, pl.ds(pl.multiple_of(tile * tq, tq), tq), :].astype(F32).T
        bound = None
        for mi in range(2):
            rows = slice(mi * HEAD_DIM, (mi + 1) * HEAD_DIM)
            q_sq = jnp.max(jnp.sum(qf[rows] * qf[rows], axis=0, keepdims=True), axis=1,
                           keepdims=True)
            self_min = jnp.min(jnp.sum(qf[rows] * ktf[rows], axis=0, keepdims=True), axis=1,
                               keepdims=True)
            k_sq = kmax_ref[:, mi:mi + 1]
            w_map = NORM_SLACK * jnp.sqrt(q_sq * k_sq) - self_min + SKIP_MARGIN * LOG2E
            bound = w_map if bound is None else jnp.maximum(bound, w_map)
        first_query = jnp.full((1, 1), tile * tq, jnp.int32).astype(F32)
        skip_f = jnp.floor((first_query - bound * inv_slope) * (1.0 / tk))
        first_block = jnp.clip(jnp.clip(skip_f, 0.0, float(2 ** 20)).astype(jnp.int32)[0, 0], 0,
                               tile * (tq // tk))
        zeros = jnp.zeros((HEAD_DIM, tq), BF16)
        row = lax.broadcasted_iota(jnp.int32, (V_DIM, tq), 0)
        ones = jnp.where(row < BIAS_TERMS, 1.0, 0.0).astype(BF16)
        weights[:, 0:tq] = jnp.concatenate([qt[0:HEAD_DIM], zeros, ones], axis=0)
        weights[:, tq:2 * tq] = jnp.concatenate([zeros, qt[HEAD_DIM:V_DIM], ones], axis=0)
        first_block_ref[0] = first_block
        scores(first_block, buf_a, weights)

    @pl.when(qi == 0)
    def _():
        open_tile(qi, qt_ref[0], w_ref)

    @pl.when(qi > 0)
    def _():
        w_ref[...] = w_next_ref[...]

    j0 = first_block_ref[0]
    acc_ref[...] = jnp.zeros(acc_ref.shape, F32)
    p_b[...] = jnp.zeros(p_b.shape, BF16)


    def softmax(buf, diag, m, token):
        s_buf, p_buf = buf
        m_out, alpha_out = [], []
        for mi in range(2):
            sm = s_buf[:, mi * tq:(mi + 1) * tq]
            if diag is not None:
                key_pos = lax.broadcasted_iota(jnp.int32, (tk, tq), 0) + diag * tk
                qry_pos = lax.broadcasted_iota(jnp.int32, (tk, tq), 1)
                sm = jnp.where(key_pos <= qry_pos, sm, MASK_VALUE)
            sm3 = sm.reshape(tk // SUBLANES, SUBLANES, tq)
            blk_max = jnp.max(sm3, axis=0)
            for rot in (4, 2, 1):
                blk_max = jnp.maximum(blk_max, pltpu.roll(blk_max, rot, 0))
            m_prev = m[mi] - shift
            m_new = jnp.maximum(m_prev, blk_max)
            alpha_out.append(jnp.exp2(m_prev - m_new))
            m_out.append(m_new)
            arg = (sm3 - (m_new + 0.0 * token[mi])[None]).reshape(tk, tq)
            p_buf[mi] = jnp.exp2(arg.astype(BF16))
        return tuple(m_out), tuple(alpha_out)

    def pv_update(j, buf, alpha):
        _, p_buf = buf
        start = pl.multiple_of(j * tk, tk)
        vt = vt_ref[0, :, pl.ds(start, tk)]
        token = []
        for mi in range(2):
            pv = jnp.dot(vt, p_buf[mi], preferred_element_type=F32)
            acc3 = acc_ref[mi].reshape(V_AUG // SUBLANES, SUBLANES, tq)
            acc_ref[mi] = (alpha[mi][None] * acc3).reshape(V_AUG, tq) + pv
            token.append(pv[V_DIM:V_DIM + SUBLANES])
        return tuple(token)

    def step(j, cur, nxt, state, diag=None, prefetch=True):
        m, alpha_prev, token_cur = state
        if prefetch:
            scores(j + 1, nxt)
        token_nxt = pv_update(jnp.maximum(j - 1, 0), nxt, alpha_prev)
        m, alpha = softmax(cur, diag, m, token_cur)
        return m, alpha, token_nxt

    def finish(cur, nxt, state):
        for d in range(n_diag):
            state = step(first_diag + d, cur, nxt, state, diag=d, prefetch=d + 1 < n_diag)
            cur, nxt = nxt, cur
        open_tile(jnp.minimum(qi + 1, pl.num_programs(2) - 1), qt_next_ref[0], w_next_ref)
        pv_update(first_diag + n_diag - 1, nxt, state[1])

    n_diag = tq // tk
    first_diag = qi * n_diag
    n_full = first_diag - j0

    def two_steps(t, state):
        state = step(j0 + 2 * t, buf_a, buf_b, state)
        return step(j0 + 2 * t + 1, buf_b, buf_a, state)

    row_init = lambda v: (jnp.full((SUBLANES, tq), v, F32), jnp.full((SUBLANES, tq), v, F32))
    state = lax.fori_loop(0, n_full // 2, two_steps,
                          (row_init(MASK_VALUE), row_init(1.0), row_init(0.0)))

    @pl.when(n_full % 2 == 1)
    def _():
        finish(buf_b, buf_a, step(first_diag - 1, buf_a, buf_b, state))

    @pl.when(n_full % 2 == 0)
    def _():
        finish(buf_a, buf_b, state)

    lam = (jnp.exp(jnp.sum(lq1_ref[...] * lk1_ref[...], keepdims=True))
           - jnp.exp(jnp.sum(lq2_ref[...] * lk2_ref[...], keepdims=True)) + LAM_INIT)
    o1 = acc_ref[0, 0:V_DIM, :] * (1.0 / acc_ref[0, V_DIM:V_DIM + 1, :])
    o2 = acc_ref[1, 0:V_DIM, :] * (1.0 / acc_ref[1, V_DIM:V_DIM + 1, :])
    ot = o1 - lam * o2
    inv_rms = lax.rsqrt(jnp.mean(ot * ot, axis=0, keepdims=True) + NORM_EPS)
    gain = jnp.tile(subln_ref[...], (1, tq // V_DIM))
    o_ref[0] = (ot * inv_rms * gain * (1.0 - LAM_INIT)).T.astype(BF16)


def _alibi_tables():
    slopes = np.asarray([2.0 ** (-8.0 * (h + 1) / ATTN_HEADS) for h in range(ATTN_HEADS)]) * LOG2E
    rest = slopes[:, None] * np.arange(ATTN_TK, dtype=np.float64)[None, :]
    tile = np.zeros((ATTN_HEADS, ATTN_TK, V_DIM), np.float32)
    for term in range(BIAS_TERMS):
        part = rest.astype(jnp.bfloat16)
        tile[:, :, term] = part.astype(np.float32)
        rest = rest - part.astype(np.float64)
    scalars = np.stack([slopes * ATTN_TK, 1.0 / slopes]).astype(np.float32)
    return jnp.asarray(tile, BF16), jnp.asarray(scalars)


def _attn_call(qt, k, vt, lq1, lk1, lq2, lk2, subln):
    b, s, d = k.shape
    tq, tk = ATTN_TQ, ATTN_TK
    bias, shifts = _alibi_tables()
    vec = lambda n: pl.BlockSpec((1, n), lambda bi, h, i: (0, 0))
    return pl.pallas_call(
        _attn_kernel,
        grid=(b, ATTN_HEADS, s // tq),
        in_specs=[
            pl.BlockSpec(memory_space=pltpu.SMEM),
            pl.BlockSpec((1, V_DIM, tq), lambda bi, h, i: (bi, h, i)),
            pl.BlockSpec((1, V_DIM, tq), lambda bi, h, i: (bi, h, jnp.minimum(i + 1, s // tq - 1))),
            pl.BlockSpec((1, s, V_DIM), lambda bi, h, i: (bi, 0, h)),
            pl.BlockSpec((1, V_AUG, s), lambda bi, h, i: (bi, h, 0)),
            pl.BlockSpec((1, tk, V_DIM), lambda bi, h, i: (h, 0, 0)),
            vec(HEAD_DIM), vec(HEAD_DIM), vec(HEAD_DIM), vec(HEAD_DIM),
            pl.BlockSpec((V_DIM, V_DIM), lambda bi, h, i: (0, 0)),
        ],
        out_specs=pl.BlockSpec((1, tq, V_DIM), lambda bi, h, i: (bi, i, h)),
        out_shape=jax.ShapeDtypeStruct((b, s, d), BF16),
        scratch_shapes=[
            pltpu.VMEM((2 * V_DIM, 2 * tq), BF16),
            pltpu.VMEM((2 * V_DIM, 2 * tq), BF16),
            pltpu.VMEM((2, V_AUG, tq), F32),
            pltpu.VMEM((tk, 2 * tq), F32),
            pltpu.VMEM((tk, 2 * tq), F32),
            pltpu.VMEM((2, tk, tq), BF16),
            pltpu.VMEM((2, tk, tq), BF16),
            pltpu.VMEM((1, V_DIM), F32),
            pltpu.SMEM((1,), jnp.int32),
        ],
        compiler_params=pltpu.CompilerParams(
            dimension_semantics=("parallel", "parallel", "arbitrary"),
            vmem_limit_bytes=VMEM_LIMIT_BYTES),
        name="diff_attn",
    )(shifts, qt, qt, k, vt, bias, lq1, lk1, lq2, lk2, subln)


def _ffn_kernel(*refs, tiles_per_seq, with_mixer_proj):
    if with_mixer_proj:
        mix_ref, wproj_ref, gmix_ref, *refs = refs
    (x_ref, gpre_ref, wa_ref, wg_ref, cw_ref, wd_ref, gpost_ref, o_ref,
     hn_ref, abuf_ref, carry_ref, act_ref) = refs
    rows = x_ref.shape[0]
    seq_start = (pl.program_id(0) % tiles_per_seq) == 0
    x = x_ref[...]
    if with_mixer_proj:
        proj = jnp.dot(mix_ref[...], wproj_ref[...], preferred_element_type=F32)
        x = x + _rms(proj, gmix_ref[...])
    hn_ref[...] = _rms(x, gpre_ref[...]).astype(BF16)
    for c in range(FFN_DIM // FFN_CHUNK):
        cols = slice(c * FFN_CHUNK, (c + 1) * FFN_CHUNK)
        a = jnp.dot(hn_ref[...], wa_ref[:, cols], preferred_element_type=F32)
        abuf_ref[0:SUBLANES, :] = jnp.where(seq_start, 0.0, carry_ref[c])
        abuf_ref[SUBLANES:SUBLANES + rows, :] = a
        carry_ref[c] = a[rows - SUBLANES:rows, :]
        cw = cw_ref[:, cols]
        a_conv = (cw[3:4] + abuf_ref[SUBLANES - 2:SUBLANES - 2 + rows, :] * cw[0:1]
                  + abuf_ref[SUBLANES - 1:SUBLANES - 1 + rows, :] * cw[1:2] + a * cw[2:3])
        g = jnp.dot(hn_ref[...], wg_ref[:, cols], preferred_element_type=F32)
        act_ref[:, cols] = (_gelu_tanh(a_conv) * g).astype(BF16)
    f = jnp.dot(act_ref[...], wd_ref[...], preferred_element_type=F32)
    o_ref[...] = x + _rms(f, gpost_ref[...])


def _ffn_call(x2d, seq_len, gpre, w_up, conv_tab, w_down, gpost, mixer_proj=None):
    t, d = x2d.shape
    rows = ROW_TILE if mixer_proj is None else ROW_TILE_WITH_PROJ
    const = lambda i: (0, 0)
    resident = pl.Buffered(1)
    mixer_specs = [] if mixer_proj is None else [
        pl.BlockSpec((rows, d), lambda i: (i, 0)),
        pl.BlockSpec((d, d), const, pipeline_mode=resident),
        pl.BlockSpec((1, d), const),
    ]
    return pl.pallas_call(
        functools.partial(_ffn_kernel, tiles_per_seq=seq_len // rows,
                          with_mixer_proj=mixer_proj is not None),
        grid=(t // rows,),
        in_specs=mixer_specs + [
            pl.BlockSpec((rows, d), lambda i: (i, 0)),
            pl.BlockSpec((1, d), const),
            pl.BlockSpec((d, FFN_DIM), lambda i: (0, 0), pipeline_mode=resident),
            pl.BlockSpec((d, FFN_DIM), lambda i: (0, 1), pipeline_mode=resident),
            pl.BlockSpec((SUBLANES, FFN_DIM), const),
            pl.BlockSpec((FFN_DIM, d), const, pipeline_mode=resident),
            pl.BlockSpec((1, d), const),
        ],
        out_specs=pl.BlockSpec((rows, d), lambda i: (i, 0)),
        out_shape=jax.ShapeDtypeStruct((t, d), F32),
        scratch_shapes=[
            pltpu.VMEM((rows, d), BF16),
            pltpu.VMEM((rows + SUBLANES, FFN_CHUNK), F32),
            pltpu.VMEM((FFN_DIM // FFN_CHUNK, SUBLANES, FFN_CHUNK), F32),
            pltpu.VMEM((rows, FFN_DIM), BF16),
        ],
        compiler_params=pltpu.CompilerParams(
            dimension_semantics=("arbitrary",),
            vmem_limit_bytes=VMEM_LIMIT_BYTES),
        name="conv_ffn",
    )(*(mixer_proj or ()), x2d, gpre, w_up, w_up, conv_tab, w_down, gpost)


def _sgu_kernel(x_ref, gpre_ref, win_ref, lng_ref, lnb_ref, ws_ref, bst_ref, wout_ref,
                gpost_ref, o_ref, hn_ref, v_ref, vn_ref, y_ref):
    rows = x_ref.shape[0]
    gd = SGU_GROUP_DIM
    x = x_ref[...]
    hn_ref[...] = _rms(x, gpre_ref[...]).astype(BF16)
    for g in range(SGU_GROUPS):
        cols = slice(SGU_WIDTH + g * gd, SGU_WIDTH + (g + 1) * gd)
        v_ref[:, g * gd:(g + 1) * gd] = _gelu_tanh(
            jnp.dot(hn_ref[...], win_ref[:, cols], preferred_element_type=F32))
    v = v_ref[...]
    mu = jnp.mean(v, axis=-1, keepdims=True)
    vc = v - mu
    var = jnp.mean(vc * vc, axis=-1, keepdims=True)
    vn_ref[...] = (vc * lax.rsqrt(var + LN_EPS) * lng_ref[...] + lnb_ref[...]).astype(BF16)
    t_pos = lax.broadcasted_iota(jnp.int32, (CHUNK, CHUNK), 0)
    s_pos = lax.broadcasted_iota(jnp.int32, (CHUNK, CHUNK), 1)
    for g in range(SGU_GROUPS):
        cols = slice(g * gd, (g + 1) * gd)
        u = _gelu_tanh(jnp.dot(hn_ref[...], win_ref[:, cols], preferred_element_type=F32))
        w = jnp.where(s_pos <= t_pos, ws_ref[g], 0.0).astype(BF16)
        bias = bst_ref[:, g:g + 1]
        for c in range(rows // CHUNK):
            r = slice(c * CHUNK, (c + 1) * CHUNK)
            s = jnp.dot(w, vn_ref[r, cols], preferred_element_type=F32) + bias
            y_ref[r, cols] = (u[r] * s).astype(BF16)
    out = jnp.dot(y_ref[...], wout_ref[...], preferred_element_type=F32)
    o_ref[...] = x + _rms(out, gpost_ref[...])


def _sgu_call(x2d, gpre, w_in, ln_g, ln_b, w_s, b_st, w_out, gpost):
    t, d = x2d.shape
    rows = ROW_TILE
    const = lambda i: (0, 0)
    resident = pl.Buffered(1)
    return pl.pallas_call(
        _sgu_kernel,
        grid=(t // rows,),
        in_specs=[
            pl.BlockSpec((rows, d), lambda i: (i, 0)),
            pl.BlockSpec((1, d), const),
            pl.BlockSpec((d, 2 * SGU_WIDTH), const, pipeline_mode=resident),
            pl.BlockSpec((1, SGU_WIDTH), const),
            pl.BlockSpec((1, SGU_WIDTH), const),
            pl.BlockSpec((SGU_GROUPS, CHUNK, CHUNK), lambda i: (0, 0, 0)),
            pl.BlockSpec((CHUNK, SGU_GROUPS), const),
            pl.BlockSpec((SGU_WIDTH, d), const, pipeline_mode=resident),
            pl.BlockSpec((1, d), const),
        ],
        out_specs=pl.BlockSpec((rows, d), lambda i: (i, 0)),
        out_shape=jax.ShapeDtypeStruct((t, d), F32),
        scratch_shapes=[
            pltpu.VMEM((rows, d), BF16),
            pltpu.VMEM((rows, SGU_WIDTH), F32),
            pltpu.VMEM((rows, SGU_WIDTH), BF16),
            pltpu.VMEM((rows, SGU_WIDTH), BF16),
        ],
        compiler_params=pltpu.CompilerParams(
            dimension_semantics=("parallel",),
            vmem_limit_bytes=VMEM_LIMIT_BYTES),
        name="chunked_sgu",
    )(x2d, gpre, w_in, ln_g, ln_b, w_s, b_st, w_out, gpost)


def _conv_table(conv_w, conv_b):
    pad = jnp.zeros((SUBLANES - 4, FFN_DIM), F32)
    return jnp.concatenate([conv_w, conv_b[None, :], pad], axis=0)


def _row(v):
    return v.reshape(1, -1)


def kernel(x, attn_w_qkv, attn_lambda_q1, attn_lambda_k1, attn_lambda_q2, attn_lambda_k2, attn_subln, attn_w_o, sgu_w_in, sgu_ln_g, sgu_ln_b, sgu_w_s, sgu_b_s, sgu_w_out, norm_mix_pre, norm_mix_post, norm_ffn_pre, norm_ffn_post, ffn_w_up, ffn_conv_w, ffn_conv_b, ffn_w_down):
    b, s, d = x.shape
    assert d == D_MODEL and s % max(QKV_ROWS, ROW_TILE, ATTN_TQ) == 0 and ATTN_TQ % ATTN_TK == 0

    w_qkv = attn_w_qkv[0]
    wqt = (w_qkv[:, :d].T * (HEAD_DIM ** -0.5 * LOG2E)).astype(BF16)
    wk = w_qkv[:, d:2 * d].astype(BF16)
    wvt = w_qkv[:, 2 * d:].T.astype(BF16)
    qt, k, vt = _qkv_call(x, _row(norm_mix_pre[0]), wqt, wk, wvt)
    attn = _attn_call(qt, k, vt, _row(attn_lambda_q1[0]), _row(attn_lambda_k1[0]),
                      _row(attn_lambda_q2[0]), _row(attn_lambda_k2[0]),
                      jnp.broadcast_to(attn_subln[0][:, None], (V_DIM, V_DIM)))
    x2d = x.reshape(b * s, d)
    x2d = _ffn_call(x2d, s, _row(norm_ffn_pre[0]), ffn_w_up[0].astype(BF16),
                    _conv_table(ffn_conv_w[0], ffn_conv_b[0]), ffn_w_down[0].astype(BF16),
                    _row(norm_ffn_post[0]),
                    mixer_proj=(attn.reshape(b * s, d), attn_w_o[0].astype(BF16),
                                _row(norm_mix_post[0])))

    x2d = _sgu_call(x2d, _row(norm_mix_pre[1]), sgu_w_in[0].astype(BF16), _row(sgu_ln_g[0]),
                    _row(sgu_ln_b[0]), sgu_w_s[0], sgu_b_s[0].T, sgu_w_out[0].astype(BF16),
                    _row(norm_mix_post[1]))
    x2d = _ffn_call(x2d, s, _row(norm_ffn_pre[1]), ffn_w_up[1].astype(BF16),
                    _conv_table(ffn_conv_w[1], ffn_conv_b[1]), ffn_w_down[1].astype(BF16),
                    _row(norm_ffn_post[1]))
    return x2d.reshape(b, s, d)
```

```python
import functools
import math

import jax
import jax.numpy as jnp
import numpy as np
from jax import lax
from jax.experimental import pallas as pl
from jax.experimental.pallas import tpu as pltpu

F32 = jnp.float32
BF16 = jnp.bfloat16

D_MODEL = 1024
ATTN_HEADS = 8
HEAD_DIM = 64
V_DIM = 2 * HEAD_DIM
SGU_WIDTH = 2 * D_MODEL
SGU_GROUPS = 8
SGU_GROUP_DIM = SGU_WIDTH // SGU_GROUPS
CHUNK = 128
FFN_DIM = 2816
NORM_EPS = 1e-6
LN_EPS = 1e-5
LAM_INIT = 0.8 - 0.6 * math.exp(-0.3 * 0)

MXU_DIM = 256
SUBLANES = 8
V_PAD = 16
V_AUG = V_DIM + V_PAD
LOG2E = math.log2(math.e)
BIAS_TERMS = 4
VMEM_LIMIT_BYTES = 56 * 1024 * 1024

QKV_ROWS = 512
ATTN_TQ = 512
ATTN_TK = 512
ROW_TILE = 1024
ROW_TILE_WITH_PROJ = 512
FFN_CHUNK = MXU_DIM
MASK_VALUE = -1e30
KNORM_ROWS = 1024
SKIP_MARGIN = 110.0
NORM_SLACK = 1.02

_NT_DIMS = (((1,), (1,)), ((), ()))


def _rms(x, gain):
    return x * lax.rsqrt(jnp.mean(x * x, axis=-1, keepdims=True) + NORM_EPS) * gain


def _gelu_tanh(x):
    return x * (0.5 * (1.0 + jnp.tanh(0.7978845608028654 * (x + 0.044715 * (x * x * x)))))


def _qkv_kernel(x_ref, g_ref, wqt_ref, wk_ref, wvt_ref, qt_ref, k_ref, vt_ref):
    rows = x_ref.shape[1]
    hn = _rms(x_ref[0], g_ref[...]).astype(BF16)
    k_ref[0] = jnp.dot(hn, wk_ref[...], preferred_element_type=F32).astype(BF16)
    qt_ref[0] = lax.dot_general(wqt_ref[...], hn, _NT_DIMS,
                                preferred_element_type=F32).astype(BF16)
    vt = lax.dot_general(wvt_ref[...], hn, _NT_DIMS,
                         preferred_element_type=F32).astype(BF16)
    pad_row = lax.broadcasted_iota(jnp.int32, (V_PAD, rows), 0)
    ones_rows = jnp.where(pad_row == 0, 1.0, 0.0).astype(BF16)
    for h in range(ATTN_HEADS):
        vt_ref[0, h * V_AUG:h * V_AUG + V_DIM, :] = vt[h * V_DIM:(h + 1) * V_DIM]
        vt_ref[0, h * V_AUG + V_DIM:(h + 1) * V_AUG, :] = ones_rows


def _qkv_call(x, gain, wqt, wk, wvt):
    b, s, d = x.shape
    rows = QKV_ROWS
    const = lambda bi, i: (0, 0)
    return pl.pallas_call(
        _qkv_kernel,
        grid=(b, s // rows),
        in_specs=[
            pl.BlockSpec((1, rows, d), lambda bi, i: (bi, i, 0)),
            pl.BlockSpec((1, d), const),
            pl.BlockSpec((d, d), const),
            pl.BlockSpec((d, d), const),
            pl.BlockSpec((d, d), const),
        ],
        out_specs=[
            pl.BlockSpec((1, d, rows), lambda bi, i: (bi, 0, i)),
            pl.BlockSpec((1, rows, d), lambda bi, i: (bi, i, 0)),
            pl.BlockSpec((1, ATTN_HEADS * V_AUG, rows), lambda bi, i: (bi, 0, i)),
        ],
        out_shape=[
            jax.ShapeDtypeStruct((b, d, s), BF16),
            jax.ShapeDtypeStruct((b, s, d), BF16),
            jax.ShapeDtypeStruct((b, ATTN_HEADS * V_AUG, s), BF16),
        ],
        compiler_params=pltpu.CompilerParams(
            dimension_semantics=("parallel", "parallel"),
            vmem_limit_bytes=VMEM_LIMIT_BYTES),
        name="qkv_proj",
    )(x, gain, wqt, wk, wvt)


def _attn_kernel(shift_ref, qt_ref, qt_next_ref, k_ref, vt_ref, bias_ref, lq1_ref, lk1_ref,
                 lq2_ref, lk2_ref, subln_ref, o_ref, w_ref, w_next_ref, acc_ref, s_a, s_b,
                 p_a, p_b, kmax_ref, first_block_ref):
    tq, tk = ATTN_TQ, ATTN_TK
    buf_a = (s_a, p_a)
    buf_b = (s_b, p_b)
    head = pl.program_id(1)
    qi = pl.program_id(2)
    shift = shift_ref[0, head]
    inv_slope = shift_ref[1, head]

    @pl.when(qi == 0)
    def _():
        dim = lax.broadcasted_iota(jnp.int32, (V_DIM, V_DIM), 0)
        col = lax.broadcasted_iota(jnp.int32, (V_DIM, V_DIM), 1)
        map_of_dim = jnp.where(dim >= HEAD_DIM, 1, 0)
        select = jnp.where(col == map_of_dim, 1.0, 0.0).astype(BF16)

        def chunk_max(c, best):
            kk = k_ref[0, pl.ds(pl.multiple_of(c * KNORM_ROWS, KNORM_ROWS), KNORM_ROWS), :]
            norms = jnp.dot(kk * kk, select, preferred_element_type=F32)
            return jnp.maximum(best, jnp.max(norms, axis=0, keepdims=True))

        kmax_ref[...] = lax.fori_loop(0, k_ref.shape[1] // KNORM_ROWS, chunk_max,
                                      jnp.zeros((1, V_DIM), F32))

    def scores(j, buf, weights=w_ref):
        start = pl.multiple_of(j * tk, tk)
        k_aug = jnp.concatenate([k_ref[0, pl.ds(start, tk), :], bias_ref[0]], axis=1)
        buf[0][...] = jnp.dot(k_aug, weights[...], preferred_element_type=F32)

    def open_tile(tile, qt, weights):
        qf = qt.astype(F32)
        ktf = k_ref[0, pl.ds(pl.multiple_of(tile * tq, tq), tq), :].astype(F32).T
        bound = None
        for mi in range(2):
            rows = slice(mi * HEAD_DIM, (mi + 1) * HEAD_DIM)
            q_sq = jnp.max(jnp.sum(qf[rows] * qf[rows], axis=0, keepdims=True), axis=1,
                           keepdims=True)
            self_min = jnp.min(jnp.sum(qf[rows] * ktf[rows], axis=0, keepdims=True), axis=1,
                               keepdims=True)
            k_sq = kmax_ref[:, mi:mi + 1]
            w_map = NORM_SLACK * jnp.sqrt(q_sq * k_sq) - self_min + SKIP_MARGIN * LOG2E
            bound = w_map if bound is None else jnp.maximum(bound, w_map)
        first_query = jnp.full((1, 1), tile * tq, jnp.int32).astype(F32)
        skip_f = jnp.floor((first_query - bound * inv_slope) * (1.0 / tk))
        first_block = jnp.clip(jnp.clip(skip_f, 0.0, float(2 ** 20)).astype(jnp.int32)[0, 0], 0,
                               tile * (tq // tk))
        zeros = jnp.zeros((HEAD_DIM, tq), BF16)
        row = lax.broadcasted_iota(jnp.int32, (V_DIM, tq), 0)
        ones = jnp.where(row < BIAS_TERMS, 1.0, 0.0).astype(BF16)
        weights[:, 0:tq] = jnp.concatenate([qt[0:HEAD_DIM], zeros, ones], axis=0)
        weights[:, tq:2 * tq] = jnp.concatenate([zeros, qt[HEAD_DIM:V_DIM], ones], axis=0)
        first_block_ref[0] = first_block
        scores(first_block, buf_a, weights)

    @pl.when(qi == 0)
    def _():
        open_tile(qi, qt_ref[0], w_ref)

    @pl.when(qi > 0)
    def _():
        w_ref[...] = w_next_ref[...]

    j0 = first_block_ref[0]
    lam = (jnp.exp(jnp.sum(lq1_ref[...] * lk1_ref[...], keepdims=True))
           - jnp.exp(jnp.sum(lq2_ref[...] * lk2_ref[...], keepdims=True)) + LAM_INIT)
    acc_ref[...] = jnp.zeros(acc_ref.shape, F32)
    p_b[...] = jnp.zeros(p_b.shape, BF16)


    def softmax(buf, diag, m, token):
        s_buf, p_buf = buf
        m_out, alpha_out = [], []
        for mi in range(2):
            sm = s_buf[:, mi * tq:(mi + 1) * tq]
            if diag is not None:
                key_pos = lax.broadcasted_iota(jnp.int32, (tk, tq), 0) + diag * tk
                qry_pos = lax.broadcasted_iota(jnp.int32, (tk, tq), 1)
                sm = jnp.where(key_pos <= qry_pos, sm, MASK_VALUE)
            sm3 = sm.reshape(tk // SUBLANES, SUBLANES, tq)
            blk_max = jnp.max(sm3, axis=0)
            for rot in (4, 2, 1):
                blk_max = jnp.maximum(blk_max, pltpu.roll(blk_max, rot, 0))
            m_prev = m[mi] - shift
            m_new = jnp.maximum(m_prev, blk_max)
            alpha_out.append(jnp.exp2(m_prev - m_new))
            m_out.append(m_new)
            arg = (sm3 - (m_new + 0.0 * token[mi])[None]).reshape(tk, tq)
            p_buf[mi] = jnp.exp2(arg.astype(BF16))
        return tuple(m_out), tuple(alpha_out)

    def pv_update(j, buf, alpha):
        _, p_buf = buf
        start = pl.multiple_of(j * tk, tk)
        vt = vt_ref[0, :, pl.ds(start, tk)]
        token = []
        for mi in range(2):
            pv = jnp.dot(vt, p_buf[mi], preferred_element_type=F32)
            acc3 = acc_ref[mi].reshape(V_AUG // SUBLANES, SUBLANES, tq)
            acc_ref[mi] = (alpha[mi][None] * acc3).reshape(V_AUG, tq) + pv
            token.append(pv[V_DIM:V_DIM + SUBLANES])
        return tuple(token)

    def step(j, cur, nxt, state, diag=None, prefetch=True):
        m, alpha_prev, token_cur = state
        if prefetch:
            scores(j + 1, nxt)
        token_nxt = pv_update(jnp.maximum(j - 1, 0), nxt, alpha_prev)
        m, alpha = softmax(cur, diag, m, token_cur)
        return m, alpha, token_nxt

    def finish(cur, nxt, state):
        for d in range(n_diag):
            state = step(first_diag + d, cur, nxt, state, diag=d, prefetch=d + 1 < n_diag)
            cur, nxt = nxt, cur
        open_tile(jnp.minimum(qi + 1, pl.num_programs(2) - 1), qt_next_ref[0], w_next_ref)
        pv_update(first_diag + n_diag - 1, nxt, state[1])

    n_diag = tq // tk
    first_diag = qi * n_diag
    n_full = first_diag - j0

    def two_steps(t, state):
        state = step(j0 + 2 * t, buf_a, buf_b, state)
        return step(j0 + 2 * t + 1, buf_b, buf_a, state)

    row_init = lambda v: (jnp.full((SUBLANES, tq), v, F32), jnp.full((SUBLANES, tq), v, F32))
    state = lax.fori_loop(0, n_full // 2, two_steps,
                          (row_init(MASK_VALUE), row_init(1.0), row_init(0.0)))

    @pl.when(n_full % 2 == 1)
    def _():
        finish(buf_b, buf_a, step(first_diag - 1, buf_a, buf_b, state))

    @pl.when(n_full % 2 == 0)
    def _():
        finish(buf_a, buf_b, state)

    o1 = acc_ref[0, 0:V_DIM, :] * (1.0 / acc_ref[0, V_DIM:V_DIM + 1, :])
    o2 = acc_ref[1, 0:V_DIM, :] * (1.0 / acc_ref[1, V_DIM:V_DIM + 1, :])
    ot = o1 - lam * o2
    inv_rms = lax.rsqrt(jnp.mean(ot * ot, axis=0, keepdims=True) + NORM_EPS)
    gain = jnp.tile(subln_ref[...], (1, tq // V_DIM))
    o_ref[0] = (ot * inv_rms * gain * (1.0 - LAM_INIT)).T.astype(BF16)


def _alibi_tables():
    slopes = np.asarray([2.0 ** (-8.0 * (h + 1) / ATTN_HEADS) for h in range(ATTN_HEADS)]) * LOG2E
    rest = slopes[:, None] * np.arange(ATTN_TK, dtype=np.float64)[None, :]
    tile = np.zeros((ATTN_HEADS, ATTN_TK, V_DIM), np.float32)
    for term in range(BIAS_TERMS):
        part = rest.astype(jnp.bfloat16)
        tile[:, :, term] = part.astype(np.float32)
        rest = rest - part.astype(np.float64)
    scalars = np.stack([slopes * ATTN_TK, 1.0 / slopes]).astype(np.float32)
    return jnp.asarray(tile, BF16), jnp.asarray(scalars)


def _attn_call(qt, k, vt, lq1, lk1, lq2, lk2, subln):
    b, s, d = k.shape
    tq, tk = ATTN_TQ, ATTN_TK
    bias, shifts = _alibi_tables()
    vec = lambda n: pl.BlockSpec((1, n), lambda bi, h, i: (0, 0))
    return pl.pallas_call(
        _attn_kernel,
        grid=(b, ATTN_HEADS, s // tq),
        in_specs=[
            pl.BlockSpec(memory_space=pltpu.SMEM),
            pl.BlockSpec((1, V_DIM, tq), lambda bi, h, i: (bi, h, i)),
            pl.BlockSpec((1, V_DIM, tq), lambda bi, h, i: (bi, h, jnp.minimum(i + 1, s // tq - 1))),
            pl.BlockSpec((1, s, V_DIM), lambda bi, h, i: (bi, 0, h)),
            pl.BlockSpec((1, V_AUG, s), lambda bi, h, i: (bi, h, 0)),
            pl.BlockSpec((1, tk, V_DIM), lambda bi, h, i: (h, 0, 0)),
            vec(HEAD_DIM), vec(HEAD_DIM), vec(HEAD_DIM), vec(HEAD_DIM),
            pl.BlockSpec((V_DIM, V_DIM), lambda bi, h, i: (0, 0)),
        ],
        out_specs=pl.BlockSpec((1, tq, V_DIM), lambda bi, h, i: (bi, i, h)),
        out_shape=jax.ShapeDtypeStruct((b, s, d), BF16),
        scratch_shapes=[
            pltpu.VMEM((2 * V_DIM, 2 * tq), BF16),
            pltpu.VMEM((2 * V_DIM, 2 * tq), BF16),
            pltpu.VMEM((2, V_AUG, tq), F32),
            pltpu.VMEM((tk, 2 * tq), F32),
            pltpu.VMEM((tk, 2 * tq), F32),
            pltpu.VMEM((2, tk, tq), BF16),
            pltpu.VMEM((2, tk, tq), BF16),
            pltpu.VMEM((1, V_DIM), F32),
            pltpu.SMEM((1,), jnp.int32),
        ],
        compiler_params=pltpu.CompilerParams(
            dimension_semantics=("parallel", "parallel", "arbitrary"),
            vmem_limit_bytes=VMEM_LIMIT_BYTES),
        name="diff_attn",
    )(shifts, qt, qt, k, vt, bias, lq1, lk1, lq2, lk2, subln)


def _ffn_kernel(*refs, tiles_per_seq, with_mixer_proj):
    if with_mixer_proj:
        mix_ref, wproj_ref, gmix_ref, *refs = refs
    (x_ref, gpre_ref, wa_ref, wg_ref, cw_ref, wd_ref, gpost_ref, o_ref,
     hn_ref, abuf_ref, carry_ref, act_ref) = refs
    rows = x_ref.shape[0]
    seq_start = (pl.program_id(0) % tiles_per_seq) == 0
    x = x_ref[...]
    if with_mixer_proj:
        proj = jnp.dot(mix_ref[...], wproj_ref[...], preferred_element_type=F32)
        x = x + _rms(proj, gmix_ref[...])
    hn_ref[...] = _rms(x, gpre_ref[...]).astype(BF16)
    for c in range(FFN_DIM // FFN_CHUNK):
        cols = slice(c * FFN_CHUNK, (c + 1) * FFN_CHUNK)
        a = jnp.dot(hn_ref[...], wa_ref[:, cols], preferred_element_type=F32)
        abuf_ref[0:SUBLANES, :] = jnp.where(seq_start, 0.0, carry_ref[c])
        abuf_ref[SUBLANES:SUBLANES + rows, :] = a
        carry_ref[c] = a[rows - SUBLANES:rows, :]
        cw = cw_ref[:, cols]
        a_conv = (cw[3:4] + abuf_ref[SUBLANES - 2:SUBLANES - 2 + rows, :] * cw[0:1]
                  + abuf_ref[SUBLANES - 1:SUBLANES - 1 + rows, :] * cw[1:2] + a * cw[2:3])
        g = jnp.dot(hn_ref[...], wg_ref[:, cols], preferred_element_type=F32)
        act_ref[:, cols] = (_gelu_tanh(a_conv) * g).astype(BF16)
    f = jnp.dot(act_ref[...], wd_ref[...], preferred_element_type=F32)
    o_ref[...] = x + _rms(f, gpost_ref[...])


def _ffn_call(x2d, seq_len, gpre, w_up, conv_tab, w_down, gpost, mixer_proj=None):
    t, d = x2d.shape
    rows = ROW_TILE if mixer_proj is None else ROW_TILE_WITH_PROJ
    const = lambda i: (0, 0)
    resident = pl.Buffered(1)
    mixer_specs = [] if mixer_proj is None else [
        pl.BlockSpec((rows, d), lambda i: (i, 0)),
        pl.BlockSpec((d, d), const, pipeline_mode=resident),
        pl.BlockSpec((1, d), const),
    ]
    return pl.pallas_call(
        functools.partial(_ffn_kernel, tiles_per_seq=seq_len // rows,
                          with_mixer_proj=mixer_proj is not None),
        grid=(t // rows,),
        in_specs=mixer_specs + [
            pl.BlockSpec((rows, d), lambda i: (i, 0)),
            pl.BlockSpec((1, d), const),
            pl.BlockSpec((d, FFN_DIM), lambda i: (0, 0), pipeline_mode=resident),
            pl.BlockSpec((d, FFN_DIM), lambda i: (0, 1), pipeline_mode=resident),
            pl.BlockSpec((SUBLANES, FFN_DIM), const),
            pl.BlockSpec((FFN_DIM, d), const, pipeline_mode=resident),
            pl.BlockSpec((1, d), const),
        ],
        out_specs=pl.BlockSpec((rows, d), lambda i: (i, 0)),
        out_shape=jax.ShapeDtypeStruct((t, d), F32),
        scratch_shapes=[
            pltpu.VMEM((rows, d), BF16),
            pltpu.VMEM((rows + SUBLANES, FFN_CHUNK), F32),
            pltpu.VMEM((FFN_DIM // FFN_CHUNK, SUBLANES, FFN_CHUNK), F32),
            pltpu.VMEM((rows, FFN_DIM), BF16),
        ],
        compiler_params=pltpu.CompilerParams(
            dimension_semantics=("arbitrary",),
            vmem_limit_bytes=VMEM_LIMIT_BYTES),
        name="conv_ffn",
    )(*(mixer_proj or ()), x2d, gpre, w_up, w_up, conv_tab, w_down, gpost)


def _sgu_kernel(x_ref, gpre_ref, win_ref, lng_ref, lnb_ref, ws_ref, bst_ref, wout_ref,
                gpost_ref, o_ref, hn_ref, v_ref, vn_ref, y_ref):
    rows = x_ref.shape[0]
    gd = SGU_GROUP_DIM
    x = x_ref[...]
    hn_ref[...] = _rms(x, gpre_ref[...]).astype(BF16)
    for g in range(SGU_GROUPS):
        cols = slice(SGU_WIDTH + g * gd, SGU_WIDTH + (g + 1) * gd)
        v_ref[:, g * gd:(g + 1) * gd] = _gelu_tanh(
            jnp.dot(hn_ref[...], win_ref[:, cols], preferred_element_type=F32))
    v = v_ref[...]
    mu = jnp.mean(v, axis=-1, keepdims=True)
    vc = v - mu
    var = jnp.mean(vc * vc, axis=-1, keepdims=True)
    vn_ref[...] = (vc * lax.rsqrt(var + LN_EPS) * lng_ref[...] + lnb_ref[...]).astype(BF16)
    t_pos = lax.broadcasted_iota(jnp.int32, (CHUNK, CHUNK), 0)
    s_pos = lax.broadcasted_iota(jnp.int32, (CHUNK, CHUNK), 1)
    for g in range(SGU_GROUPS):
        cols = slice(g * gd, (g + 1) * gd)
        u = _gelu_tanh(jnp.dot(hn_ref[...], win_ref[:, cols], preferred_element_type=F32))
        w = jnp.where(s_pos <= t_pos, ws_ref[g], 0.0).astype(BF16)
        bias = bst_ref[:, g:g + 1]
        for c in range(rows // CHUNK):
            r = slice(c * CHUNK, (c + 1) * CHUNK)
            s = jnp.dot(w, vn_ref[r, cols], preferred_element_type=F32) + bias
            y_ref[r, cols] = (u[r] * s).astype(BF16)
    out = jnp.dot(y_ref[...], wout_ref[...], preferred_element_type=F32)
    o_ref[...] = x + _rms(out, gpost_ref[...])


def _sgu_call(x2d, gpre, w_in, ln_g, ln_b, w_s, b_st, w_out, gpost):
    t, d = x2d.shape
    rows = ROW_TILE
    const = lambda i: (0, 0)
    resident = pl.Buffered(1)
    return pl.pallas_call(
        _sgu_kernel,
        grid=(t // rows,),
        in_specs=[
            pl.BlockSpec((rows, d), lambda i: (i, 0)),
            pl.BlockSpec((1, d), const),
            pl.BlockSpec((d, 2 * SGU_WIDTH), const, pipeline_mode=resident),
            pl.BlockSpec((1, SGU_WIDTH), const),
            pl.BlockSpec((1, SGU_WIDTH), const),
            pl.BlockSpec((SGU_GROUPS, CHUNK, CHUNK), lambda i: (0, 0, 0)),
            pl.BlockSpec((CHUNK, SGU_GROUPS), const),
            pl.BlockSpec((SGU_WIDTH, d), const, pipeline_mode=resident),
            pl.BlockSpec((1, d), const),
        ],
        out_specs=pl.BlockSpec((rows, d), lambda i: (i, 0)),
        out_shape=jax.ShapeDtypeStruct((t, d), F32),
        scratch_shapes=[
            pltpu.VMEM((rows, d), BF16),
            pltpu.VMEM((rows, SGU_WIDTH), F32),
            pltpu.VMEM((rows, SGU_WIDTH), BF16),
            pltpu.VMEM((rows, SGU_WIDTH), BF16),
        ],
        compiler_params=pltpu.CompilerParams(
            dimension_semantics=("parallel",),
            vmem_limit_bytes=VMEM_LIMIT_BYTES),
        name="chunked_sgu",
    )(x2d, gpre, w_in, ln_g, ln_b, w_s, b_st, w_out, gpost)


def _conv_table(conv_w, conv_b):
    pad = jnp.zeros((SUBLANES - 4, FFN_DIM), F32)
    return jnp.concatenate([conv_w, conv_b[None, :], pad], axis=0)


def _row(v):
    return v.reshape(1, -1)


def kernel(x, attn_w_qkv, attn_lambda_q1, attn_lambda_k1, attn_lambda_q2, attn_lambda_k2, attn_subln, attn_w_o, sgu_w_in, sgu_ln_g, sgu_ln_b, sgu_w_s, sgu_b_s, sgu_w_out, norm_mix_pre, norm_mix_post, norm_ffn_pre, norm_ffn_post, ffn_w_up, ffn_conv_w, ffn_conv_b, ffn_w_down):
    b, s, d = x.shape
    assert d == D_MODEL and s % max(QKV_ROWS, ROW_TILE, ATTN_TQ) == 0 and ATTN_TQ % ATTN_TK == 0

    w_qkv = attn_w_qkv[0]
    wqt = (w_qkv[:, :d].T * (HEAD_DIM ** -0.5 * LOG2E)).astype(BF16)
    wk = w_qkv[:, d:2 * d].astype(BF16)
    wvt = w_qkv[:, 2 * d:].T.astype(BF16)
    qt, k, vt = _qkv_call(x, _row(norm_mix_pre[0]), wqt, wk, wvt)
    attn = _attn_call(qt, k, vt, _row(attn_lambda_q1[0]), _row(attn_lambda_k1[0]),
                      _row(attn_lambda_q2[0]), _row(attn_lambda_k2[0]),
                      jnp.broadcast_to(attn_subln[0][:, None], (V_DIM, V_DIM)))
    x2d = x.reshape(b * s, d)
    x2d = _ffn_call(x2d, s, _row(norm_ffn_pre[0]), ffn_w_up[0].astype(BF16),
                    _conv_table(ffn_conv_w[0], ffn_conv_b[0]), ffn_w_down[0].astype(BF16),
                    _row(norm_ffn_post[0]),
                    mixer_proj=(attn.reshape(b * s, d), attn_w_o[0].astype(BF16),
                                _row(norm_mix_post[0])))

    x2d = _sgu_call(x2d, _row(norm_mix_pre[1]), sgu_w_in[0].astype(BF16), _row(sgu_ln_g[0]),
                    _row(sgu_ln_b[0]), sgu_w_s[0], sgu_b_s[0].T, sgu_w_out[0].astype(BF16),
                    _row(norm_mix_post[1]))
    x2d = _ffn_call(x2d, s, _row(norm_ffn_pre[1]), ffn_w_up[1].astype(BF16),
                    _conv_table(ffn_conv_w[1], ffn_conv_b[1]), ffn_w_down[1].astype(BF16),
                    _row(norm_ffn_post[1]))
    return x2d.reshape(b, s, d)
```

```python
import functools
import math

import jax
import jax.numpy as jnp
import numpy as np
from jax import lax
from jax.experimental import pallas as pl
from jax.experimental.pallas import tpu as pltpu

F32 = jnp.float32
BF16 = jnp.bfloat16

D_MODEL = 1024
ATTN_HEADS = 8
HEAD_DIM = 64
V_DIM = 2 * HEAD_DIM
SGU_WIDTH = 2 * D_MODEL
SGU_GROUPS = 8
SGU_GROUP_DIM = SGU_WIDTH // SGU_GROUPS
CHUNK = 128
FFN_DIM = 2816
NORM_EPS = 1e-6
LN_EPS = 1e-5
LAM_INIT = 0.8 - 0.6 * math.exp(-0.3 * 0)

MXU_DIM = 256
SUBLANES = 8
V_PAD = 16
V_AUG = V_DIM + V_PAD
LOG2E = math.log2(math.e)
BIAS_TERMS = 4
VMEM_LIMIT_BYTES = 56 * 1024 * 1024

QKV_ROWS = 512
ATTN_TQ = 512
ATTN_TK = 512
ROW_TILE = 1024
ROW_TILE_WITH_PROJ = 512
FFN_CHUNK = MXU_DIM
MASK_VALUE = -1e30
KNORM_ROWS = 1024
SKIP_MARGIN = 110.0
NORM_SLACK = 1.02

_NT_DIMS = (((1,), (1,)), ((), ()))


def _rms(x, gain):
    return x * lax.rsqrt(jnp.mean(x * x, axis=-1, keepdims=True) + NORM_EPS) * gain


def _gelu_tanh(x):
    return x * (0.5 * (1.0 + jnp.tanh(0.7978845608028654 * (x + 0.044715 * (x * x * x)))))


def _qkv_kernel(x_ref, g_ref, wqt_ref, wk_ref, wvt_ref, qt_ref, k_ref, vt_ref):
    rows = x_ref.shape[1]
    hn = _rms(x_ref[0], g_ref[...]).astype(BF16)
    k_ref[0] = jnp.dot(hn, wk_ref[...], preferred_element_type=F32).astype(BF16)
    qt_ref[0] = lax.dot_general(wqt_ref[...], hn, _NT_DIMS,
                                preferred_element_type=F32).astype(BF16)
    vt = lax.dot_general(wvt_ref[...], hn, _NT_DIMS,
                         preferred_element_type=F32).astype(BF16)
    pad_row = lax.broadcasted_iota(jnp.int32, (V_PAD, rows), 0)
    ones_rows = jnp.where(pad_row == 0, 1.0, 0.0).astype(BF16)
    for h in range(ATTN_HEADS):
        vt_ref[0, h * V_AUG:h * V_AUG + V_DIM, :] = vt[h * V_DIM:(h + 1) * V_DIM]
        vt_ref[0, h * V_AUG + V_DIM:(h + 1) * V_AUG, :] = ones_rows


def _qkv_call(x, gain, wqt, wk, wvt):
    b, s, d = x.shape
    rows = QKV_ROWS
    const = lambda bi, i: (0, 0)
    return pl.pallas_call(
        _qkv_kernel,
        grid=(b, s // rows),
        in_specs=[
            pl.BlockSpec((1, rows, d), lambda bi, i: (bi, i, 0)),
            pl.BlockSpec((1, d), const),
            pl.BlockSpec((d, d), const),
            pl.BlockSpec((d, d), const),
            pl.BlockSpec((d, d), const),
        ],
        out_specs=[
            pl.BlockSpec((1, d, rows), lambda bi, i: (bi, 0, i)),
            pl.BlockSpec((1, rows, d), lambda bi, i: (bi, i, 0)),
            pl.BlockSpec((1, ATTN_HEADS * V_AUG, rows), lambda bi, i: (bi, 0, i)),
        ],
        out_shape=[
            jax.ShapeDtypeStruct((b, d, s), BF16),
            jax.ShapeDtypeStruct((b, s, d), BF16),
            jax.ShapeDtypeStruct((b, ATTN_HEADS * V_AUG, s), BF16),
        ],
        compiler_params=pltpu.CompilerParams(
            dimension_semantics=("parallel", "parallel"),
            vmem_limit_bytes=VMEM_LIMIT_BYTES),
        name="qkv_proj",
    )(x, gain, wqt, wk, wvt)


def _attn_kernel(shift_ref, qt_ref, qt_next_ref, k_ref, vt_ref, bias_ref, lq1_ref, lk1_ref,
                 lq2_ref, lk2_ref, subln_ref, o_ref, w_ref, w_next_ref, acc_ref, s_a, s_b,
                 p_a, p_b, kmax_ref, first_block_ref):
    tq, tk = ATTN_TQ, ATTN_TK
    buf_a = (s_a, p_a)
    buf_b = (s_b, p_b)
    head = pl.program_id(1)
    qi = pl.program_id(2)
    shift = shift_ref[0, head]
    inv_slope = shift_ref[1, head]

    @pl.when(qi == 0)
    def _():
        dim = lax.broadcasted_iota(jnp.int32, (V_DIM, V_DIM), 0)
        col = lax.broadcasted_iota(jnp.int32, (V_DIM, V_DIM), 1)
        map_of_dim = jnp.where(dim >= HEAD_DIM, 1, 0)
        select = jnp.where(col == map_of_dim, 1.0, 0.0).astype(BF16)

        def chunk_max(c, best):
            kk = k_ref[0, pl.ds(pl.multiple_of(c * KNORM_ROWS, KNORM_ROWS), KNORM_ROWS), :]
            norms = jnp.dot(kk * kk, select, preferred_element_type=F32)
            return jnp.maximum(best, jnp.max(norms, axis=0, keepdims=True))

        kmax_ref[...] = lax.fori_loop(0, k_ref.shape[1] // KNORM_ROWS, chunk_max,
                                      jnp.zeros((1, V_DIM), F32))

    def scores(j, buf, weights=w_ref):
        start = pl.multiple_of(j * tk, tk)
        k_aug = jnp.concatenate([k_ref[0, pl.ds(start, tk), :], bias_ref[0]], axis=1)
        buf[0][...] = jnp.dot(k_aug, weights[...], preferred_element_type=F32)

    def open_tile(tile, qt, weights):
        qf = qt.astype(F32)
        ktf = k_ref[0, pl.ds(pl.multiple_of(tile * tq, tq), tq), :].astype(F32).T
        bound = None
        for mi in range(2):
            rows = slice(mi * HEAD_DIM, (mi + 1) * HEAD_DIM)
            q_sq = jnp.max(jnp.sum(qf[rows] * qf[rows], axis=0, keepdims=True), axis=1,
                           keepdims=True)
            self_min = jnp.min(jnp.sum(qf[rows] * ktf[rows], axis=0, keepdims=True), axis=1,
                               keepdims=True)
            k_sq = kmax_ref[:, mi:mi + 1]
            w_map = NORM_SLACK * jnp.sqrt(q_sq * k_sq) - self_min + SKIP_MARGIN * LOG2E
            bound = w_map if bound is None else jnp.maximum(bound, w_map)
        first_query = jnp.full((1, 1), tile * tq, jnp.int32).astype(F32)
        skip_f = jnp.floor((first_query - bound * inv_slope) * (1.0 / tk))
        first_block = jnp.clip(jnp.clip(skip_f, 0.0, float(2 ** 20)).astype(jnp.int32)[0, 0], 0,
                               tile * (tq // tk))
        zeros = jnp.zeros((HEAD_DIM, tq), BF16)
        row = lax.broadcasted_iota(jnp.int32, (V_DIM, tq), 0)
        ones = jnp.where(row < BIAS_TERMS, 1.0, 0.0).astype(BF16)
        weights[:, 0:tq] = jnp.concatenate([qt[0:HEAD_DIM], zeros, ones], axis=0)
        weights[:, tq:2 * tq] = jnp.concatenate([zeros, qt[HEAD_DIM:V_DIM], ones], axis=0)
        first_block_ref[0] = first_block
        scores(first_block, buf_a, weights)

    @pl.when(qi == 0)
    def _():
        open_tile(qi, qt_ref[0], w_ref)

    @pl.when(qi > 0)
    def _():
        w_ref[...] = w_next_ref[...]

    j0 = first_block_ref[0]
    acc_ref[...] = jnp.zeros(acc_ref.shape, F32)
    p_b[...] = jnp.zeros(p_b.shape, BF16)


    def softmax(buf, diag, m, token):
        s_buf, p_buf = buf
        m_out, alpha_out = [], []
        for mi in range(2):
            sm = s_buf[:, mi * tq:(mi + 1) * tq]
            if diag is not None:
                key_pos = lax.broadcasted_iota(jnp.int32, (tk, tq), 0) + diag * tk
                qry_pos = lax.broadcasted_iota(jnp.int32, (tk, tq), 1)
                sm = jnp.where(key_pos <= qry_pos, sm, MASK_VALUE)
            sm3 = sm.reshape(tk // SUBLANES, SUBLANES, tq)
            blk_max = jnp.max(sm3, axis=0)
            for rot in (4, 2, 1):
                blk_max = jnp.maximum(blk_max, pltpu.roll(blk_max, rot, 0))
            m_prev = m[mi] - shift
            m_new = jnp.maximum(m_prev, blk_max)
            alpha_out.append(jnp.exp2(m_prev - m_new))
            m_out.append(m_new)
            arg = (sm3 - (m_new + 0.0 * token[mi])[None]).reshape(tk, tq)
            p_buf[mi] = jnp.exp2(arg.astype(BF16))
        return tuple(m_out), tuple(alpha_out)

    def pv_update(j, buf, alpha):
        _, p_buf = buf
        start = pl.multiple_of(j * tk, tk)
        vt = vt_ref[0, :, pl.ds(start, tk)]
        token = []
        for mi in range(2):
            pv = jnp.dot(vt, p_buf[mi], preferred_element_type=F32)
            acc3 = acc_ref[mi].reshape(V_AUG // SUBLANES, SUBLANES, tq)
            acc_ref[mi] = (alpha[mi][None] * acc3).reshape(V_AUG, tq) + pv
            token.append(pv[V_DIM:V_DIM + SUBLANES])
        return tuple(token)

    def step(j, cur, nxt, state, diag=None, prefetch=True):
        m, alpha_prev, token_cur = state
        if prefetch:
            scores(j + 1, nxt)
        token_nxt = pv_update(jnp.maximum(j - 1, 0), nxt, alpha_prev)
        m, alpha = softmax(cur, diag, m, token_cur)
        return m, alpha, token_nxt

    def finish(cur, nxt, state):
        for d in range(n_diag):
            state = step(first_diag + d, cur, nxt, state, diag=d, prefetch=d + 1 < n_diag)
            cur, nxt = nxt, cur
        open_tile(jnp.minimum(qi + 1, pl.num_programs(2) - 1), qt_next_ref[0], w_next_ref)
        pv_update(first_diag + n_diag - 1, nxt, state[1])
        write_output()

    def write_output():
        lam = (jnp.exp(jnp.sum(lq1_ref[...] * lk1_ref[...], keepdims=True))
               - jnp.exp(jnp.sum(lq2_ref[...] * lk2_ref[...], keepdims=True)) + LAM_INIT)
        o1 = acc_ref[0, 0:V_DIM, :] * (1.0 / acc_ref[0, V_DIM:V_DIM + 1, :])
        o2 = acc_ref[1, 0:V_DIM, :] * (1.0 / acc_ref[1, V_DIM:V_DIM + 1, :])
        ot = o1 - lam * o2
        inv_rms = lax.rsqrt(jnp.mean(ot * ot, axis=0, keepdims=True) + NORM_EPS)
        gain = jnp.tile(subln_ref[...], (1, tq // V_DIM))
        o_ref[0] = (ot * inv_rms * gain * (1.0 - LAM_INIT)).T.astype(BF16)

    n_diag = tq // tk
    first_diag = qi * n_diag
    n_full = first_diag - j0

    def two_steps(t, state):
        state = step(j0 + 2 * t, buf_a, buf_b, state)
        return step(j0 + 2 * t + 1, buf_b, buf_a, state)

    row_init = lambda v: (jnp.full((SUBLANES, tq), v, F32), jnp.full((SUBLANES, tq), v, F32))
    state = lax.fori_loop(0, n_full // 2, two_steps,
                          (row_init(MASK_VALUE), row_init(1.0), row_init(0.0)))

    @pl.when(n_full % 2 == 1)
    def _():
        finish(buf_b, buf_a, step(first_diag - 1, buf_a, buf_b, state))

    @pl.when(n_full % 2 == 0)
    def _():
        finish(buf_a, buf_b, state)


def _alibi_tables():
    slopes = np.asarray([2.0 ** (-8.0 * (h + 1) / ATTN_HEADS) for h in range(ATTN_HEADS)]) * LOG2E
    rest = slopes[:, None] * np.arange(ATTN_TK, dtype=np.float64)[None, :]
    tile = np.zeros((ATTN_HEADS, ATTN_TK, V_DIM), np.float32)
    for term in range(BIAS_TERMS):
        part = rest.astype(jnp.bfloat16)
        tile[:, :, term] = part.astype(np.float32)
        rest = rest - part.astype(np.float64)
    scalars = np.stack([slopes * ATTN_TK, 1.0 / slopes]).astype(np.float32)
    return jnp.asarray(tile, BF16), jnp.asarray(scalars)


def _attn_call(qt, k, vt, lq1, lk1, lq2, lk2, subln):
    b, s, d = k.shape
    tq, tk = ATTN_TQ, ATTN_TK
    bias, shifts = _alibi_tables()
    vec = lambda n: pl.BlockSpec((1, n), lambda bi, h, i: (0, 0))
    return pl.pallas_call(
        _attn_kernel,
        grid=(b, ATTN_HEADS, s // tq),
        in_specs=[
            pl.BlockSpec(memory_space=pltpu.SMEM),
            pl.BlockSpec((1, V_DIM, tq), lambda bi, h, i: (bi, h, i)),
            pl.BlockSpec((1, V_DIM, tq), lambda bi, h, i: (bi, h, jnp.minimum(i + 1, s // tq - 1))),
            pl.BlockSpec((1, s, V_DIM), lambda bi, h, i: (bi, 0, h)),
            pl.BlockSpec((1, V_AUG, s), lambda bi, h, i: (bi, h, 0)),
            pl.BlockSpec((1, tk, V_DIM), lambda bi, h, i: (h, 0, 0)),
            vec(HEAD_DIM), vec(HEAD_DIM), vec(HEAD_DIM), vec(HEAD_DIM),
            pl.BlockSpec((V_DIM, V_DIM), lambda bi, h, i: (0, 0)),
        ],
        out_specs=pl.BlockSpec((1, tq, V_DIM), lambda bi, h, i: (bi, i, h)),
        out_shape=jax.ShapeDtypeStruct((b, s, d), BF16),
        scratch_shapes=[
            pltpu.VMEM((2 * V_DIM, 2 * tq), BF16),
            pltpu.VMEM((2 * V_DIM, 2 * tq), BF16),
            pltpu.VMEM((2, V_AUG, tq), F32),
            pltpu.VMEM((tk, 2 * tq), F32),
            pltpu.VMEM((tk, 2 * tq), F32),
            pltpu.VMEM((2, tk, tq), BF16),
            pltpu.VMEM((2, tk, tq), BF16),
            pltpu.VMEM((1, V_DIM), F32),
            pltpu.SMEM((1,), jnp.int32),
        ],
        compiler_params=pltpu.CompilerParams(
            dimension_semantics=("parallel", "parallel", "arbitrary"),
            vmem_limit_bytes=VMEM_LIMIT_BYTES),
        name="diff_attn",
    )(shifts, qt, qt, k, vt, bias, lq1, lk1, lq2, lk2, subln)


def _ffn_kernel(*refs, tiles_per_seq, with_mixer_proj):
    if with_mixer_proj:
        mix_ref, wproj_ref, gmix_ref, *refs = refs
    (x_ref, gpre_ref, wa_ref, wg_ref, cw_ref, wd_ref, gpost_ref, o_ref,
     hn_ref, abuf_ref, carry_ref, act_ref) = refs
    rows = x_ref.shape[0]
    seq_start = (pl.program_id(0) % tiles_per_seq) == 0
    x = x_ref[...]
    if with_mixer_proj:
        proj = jnp.dot(mix_ref[...], wproj_ref[...], preferred_element_type=F32)
        x = x + _rms(proj, gmix_ref[...])
    hn_ref[...] = _rms(x, gpre_ref[...]).astype(BF16)
    for c in range(FFN_DIM // FFN_CHUNK):
        cols = slice(c * FFN_CHUNK, (c + 1) * FFN_CHUNK)
        a = jnp.dot(hn_ref[...], wa_ref[:, cols], preferred_element_type=F32)
        abuf_ref[0:SUBLANES, :] = jnp.where(seq_start, 0.0, carry_ref[c])
        abuf_ref[SUBLANES:SUBLANES + rows, :] = a
        carry_ref[c] = a[rows - SUBLANES:rows, :]
        cw = cw_ref[:, cols]
        a_conv = (cw[3:4] + abuf_ref[SUBLANES - 2:SUBLANES - 2 + rows, :] * cw[0:1]
                  + abuf_ref[SUBLANES - 1:SUBLANES - 1 + rows, :] * cw[1:2] + a * cw[2:3])
        g = jnp.dot(hn_ref[...], wg_ref[:, cols], preferred_element_type=F32)
        act_ref[:, cols] = (_gelu_tanh(a_conv) * g).astype(BF16)
    f = jnp.dot(act_ref[...], wd_ref[...], preferred_element_type=F32)
    o_ref[...] = x + _rms(f, gpost_ref[...])


def _ffn_call(x2d, seq_len, gpre, w_up, conv_tab, w_down, gpost, mixer_proj=None):
    t, d = x2d.shape
    rows = ROW_TILE if mixer_proj is None else ROW_TILE_WITH_PROJ
    const = lambda i: (0, 0)
    resident = pl.Buffered(1)
    mixer_specs = [] if mixer_proj is None else [
        pl.BlockSpec((rows, d), lambda i: (i, 0)),
        pl.BlockSpec((d, d), const, pipeline_mode=resident),
        pl.BlockSpec((1, d), const),
    ]
    return pl.pallas_call(
        functools.partial(_ffn_kernel, tiles_per_seq=seq_len // rows,
                          with_mixer_proj=mixer_proj is not None),
        grid=(t // rows,),
        in_specs=mixer_specs + [
            pl.BlockSpec((rows, d), lambda i: (i, 0)),
            pl.BlockSpec((1, d), const),
            pl.BlockSpec((d, FFN_DIM), lambda i: (0, 0), pipeline_mode=resident),
            pl.BlockSpec((d, FFN_DIM), lambda i: (0, 1), pipeline_mode=resident),
            pl.BlockSpec((SUBLANES, FFN_DIM), const),
            pl.BlockSpec((FFN_DIM, d), const, pipeline_mode=resident),
            pl.BlockSpec((1, d), const),
        ],
        out_specs=pl.BlockSpec((rows, d), lambda i: (i, 0)),
        out_shape=jax.ShapeDtypeStruct((t, d), F32),
        scratch_shapes=[
            pltpu.VMEM((rows, d), BF16),
            pltpu.VMEM((rows + SUBLANES, FFN_CHUNK), F32),
            pltpu.VMEM((FFN_DIM // FFN_CHUNK, SUBLANES, FFN_CHUNK), F32),
            pltpu.VMEM((rows, FFN_DIM), BF16),
        ],
        compiler_params=pltpu.CompilerParams(
            dimension_semantics=("arbitrary",),
            vmem_limit_bytes=VMEM_LIMIT_BYTES),
        name="conv_ffn",
    )(*(mixer_proj or ()), x2d, gpre, w_up, w_up, conv_tab, w_down, gpost)


def _sgu_kernel(x_ref, gpre_ref, win_ref, lng_ref, lnb_ref, ws_ref, bst_ref, wout_ref,
                gpost_ref, o_ref, hn_ref, v_ref, vn_ref, y_ref):
    rows = x_ref.shape[0]
    gd = SGU_GROUP_DIM
    x = x_ref[...]
    hn_ref[...] = _rms(x, gpre_ref[...]).astype(BF16)
    for g in range(SGU_GROUPS):
        cols = slice(SGU_WIDTH + g * gd, SGU_WIDTH + (g + 1) * gd)
        v_ref[:, g * gd:(g + 1) * gd] = _gelu_tanh(
            jnp.dot(hn_ref[...], win_ref[:, cols], preferred_element_type=F32))
    v = v_ref[...]
    mu = jnp.mean(v, axis=-1, keepdims=True)
    vc = v - mu
    var = jnp.mean(vc * vc, axis=-1, keepdims=True)
    vn_ref[...] = (vc * lax.rsqrt(var + LN_EPS) * lng_ref[...] + lnb_ref[...]).astype(BF16)
    t_pos = lax.broadcasted_iota(jnp.int32, (CHUNK, CHUNK), 0)
    s_pos = lax.broadcasted_iota(jnp.int32, (CHUNK, CHUNK), 1)
    for g in range(SGU_GROUPS):
        cols = slice(g * gd, (g + 1) * gd)
        u = _gelu_tanh(jnp.dot(hn_ref[...], win_ref[:, cols], preferred_element_type=F32))
        w = jnp.where(s_pos <= t_pos, ws_ref[g], 0.0).astype(BF16)
        bias = bst_ref[:, g:g + 1]
        for c in range(rows // CHUNK):
            r = slice(c * CHUNK, (c + 1) * CHUNK)
            s = jnp.dot(w, vn_ref[r, cols], preferred_element_type=F32) + bias
            y_ref[r, cols] = (u[r] * s).astype(BF16)
    out = jnp.dot(y_ref[...], wout_ref[...], preferred_element_type=F32)
    o_ref[...] = x + _rms(out, gpost_ref[...])


def _sgu_call(x2d, gpre, w_in, ln_g, ln_b, w_s, b_st, w_out, gpost):
    t, d = x2d.shape
    rows = ROW_TILE
    const = lambda i: (0, 0)
    resident = pl.Buffered(1)
    return pl.pallas_call(
        _sgu_kernel,
        grid=(t // rows,),
        in_specs=[
            pl.BlockSpec((rows, d), lambda i: (i, 0)),
            pl.BlockSpec((1, d), const),
            pl.BlockSpec((d, 2 * SGU_WIDTH), const, pipeline_mode=resident),
            pl.BlockSpec((1, SGU_WIDTH), const),
            pl.BlockSpec((1, SGU_WIDTH), const),
            pl.BlockSpec((SGU_GROUPS, CHUNK, CHUNK), lambda i: (0, 0, 0)),
            pl.BlockSpec((CHUNK, SGU_GROUPS), const),
            pl.BlockSpec((SGU_WIDTH, d), const, pipeline_mode=resident),
            pl.BlockSpec((1, d), const),
        ],
        out_specs=pl.BlockSpec((rows, d), lambda i: (i, 0)),
        out_shape=jax.ShapeDtypeStruct((t, d), F32),
        scratch_shapes=[
            pltpu.VMEM((rows, d), BF16),
            pltpu.VMEM((rows, SGU_WIDTH), F32),
            pltpu.VMEM((rows, SGU_WIDTH), BF16),
            pltpu.VMEM((rows, SGU_WIDTH), BF16),
        ],
        compiler_params=pltpu.CompilerParams(
            dimension_semantics=("parallel",),
            vmem_limit_bytes=VMEM_LIMIT_BYTES),
        name="chunked_sgu",
    )(x2d, gpre, w_in, ln_g, ln_b, w_s, b_st, w_out, gpost)


def _conv_table(conv_w, conv_b):
    pad = jnp.zeros((SUBLANES - 4, FFN_DIM), F32)
    return jnp.concatenate([conv_w, conv_b[None, :], pad], axis=0)


def _row(v):
    return v.reshape(1, -1)


def kernel(x, attn_w_qkv, attn_lambda_q1, attn_lambda_k1, attn_lambda_q2, attn_lambda_k2, attn_subln, attn_w_o, sgu_w_in, sgu_ln_g, sgu_ln_b, sgu_w_s, sgu_b_s, sgu_w_out, norm_mix_pre, norm_mix_post, norm_ffn_pre, norm_ffn_post, ffn_w_up, ffn_conv_w, ffn_conv_b, ffn_w_down):
    b, s, d = x.shape
    assert d == D_MODEL and s % max(QKV_ROWS, ROW_TILE, ATTN_TQ) == 0 and ATTN_TQ % ATTN_TK == 0

    w_qkv = attn_w_qkv[0]
    wqt = (w_qkv[:, :d].T * (HEAD_DIM ** -0.5 * LOG2E)).astype(BF16)
    wk = w_qkv[:, d:2 * d].astype(BF16)
    wvt = w_qkv[:, 2 * d:].T.astype(BF16)
    qt, k, vt = _qkv_call(x, _row(norm_mix_pre[0]), wqt, wk, wvt)
    attn = _attn_call(qt, k, vt, _row(attn_lambda_q1[0]), _row(attn_lambda_k1[0]),
                      _row(attn_lambda_q2[0]), _row(attn_lambda_k2[0]),
                      jnp.broadcast_to(attn_subln[0][:, None], (V_DIM, V_DIM)))
    x2d = x.reshape(b * s, d)
    x2d = _ffn_call(x2d, s, _row(norm_ffn_pre[0]), ffn_w_up[0].astype(BF16),
                    _conv_table(ffn_conv_w[0], ffn_conv_b[0]), ffn_w_down[0].astype(BF16),
                    _row(norm_ffn_post[0]),
                    mixer_proj=(attn.reshape(b * s, d), attn_w_o[0].astype(BF16),
                                _row(norm_mix_post[0])))

    x2d = _sgu_call(x2d, _row(norm_mix_pre[1]), sgu_w_in[0].astype(BF16), _row(sgu_ln_g[0]),
                    _row(sgu_ln_b[0]), sgu_w_s[0], sgu_b_s[0].T, sgu_w_out[0].astype(BF16),
                    _row(norm_mix_post[1]))
    x2d = _ffn_call(x2d, s, _row(norm_ffn_pre[1]), ffn_w_up[1].astype(BF16),
                    _conv_table(ffn_conv_w[1], ffn_conv_b[1]), ffn_w_down[1].astype(BF16),
                    _row(norm_ffn_post[1]))
    return x2d.reshape(b, s, d)
```

```python
import functools
import math

import jax
import jax.numpy as jnp
import numpy as np
from jax import lax
from jax.experimental import pallas as pl
from jax.experimental.pallas import tpu as pltpu

F32 = jnp.float32
BF16 = jnp.bfloat16

D_MODEL = 1024
ATTN_HEADS = 8
HEAD_DIM = 64
V_DIM = 2 * HEAD_DIM
SGU_WIDTH = 2 * D_MODEL
SGU_GROUPS = 8
SGU_GROUP_DIM = SGU_WIDTH // SGU_GROUPS
CHUNK = 128
FFN_DIM = 2816
NORM_EPS = 1e-6
LN_EPS = 1e-5
LAM_INIT = 0.8 - 0.6 * math.exp(-0.3 * 0)

MXU_DIM = 256
SUBLANES = 8
V_PAD = 16
V_AUG = V_DIM + V_PAD
LOG2E = math.log2(math.e)
BIAS_TERMS = 4
VMEM_LIMIT_BYTES = 56 * 1024 * 1024

QKV_ROWS = 512
ATTN_TQ = 512
ATTN_TK = 512
SOFTMAX_LANES = 256
ROW_TILE = 1024
ROW_TILE_WITH_PROJ = 512
FFN_CHUNK = MXU_DIM
MASK_VALUE = -1e30
KNORM_ROWS = 1024
SKIP_MARGIN = 110.0
NORM_SLACK = 1.02

_NT_DIMS = (((1,), (1,)), ((), ()))


def _rms(x, gain):
    return x * lax.rsqrt(jnp.mean(x * x, axis=-1, keepdims=True) + NORM_EPS) * gain


def _gelu_tanh(x):
    return x * (0.5 * (1.0 + jnp.tanh(0.7978845608028654 * (x + 0.044715 * (x * x * x)))))


def _qkv_kernel(x_ref, g_ref, wqt_ref, wk_ref, wvt_ref, qt_ref, k_ref, vt_ref):
    rows = x_ref.shape[1]
    hn = _rms(x_ref[0], g_ref[...]).astype(BF16)
    k_ref[0] = jnp.dot(hn, wk_ref[...], preferred_element_type=F32).astype(BF16)
    qt_ref[0] = lax.dot_general(wqt_ref[...], hn, _NT_DIMS,
                                preferred_element_type=F32).astype(BF16)
    vt = lax.dot_general(wvt_ref[...], hn, _NT_DIMS,
                         preferred_element_type=F32).astype(BF16)
    pad_row = lax.broadcasted_iota(jnp.int32, (V_PAD, rows), 0)
    ones_rows = jnp.where(pad_row == 0, 1.0, 0.0).astype(BF16)
    for h in range(ATTN_HEADS):
        vt_ref[0, h * V_AUG:h * V_AUG + V_DIM, :] = vt[h * V_DIM:(h + 1) * V_DIM]
        vt_ref[0, h * V_AUG + V_DIM:(h + 1) * V_AUG, :] = ones_rows


def _qkv_call(x, gain, wqt, wk, wvt):
    b, s, d = x.shape
    rows = QKV_ROWS
    const = lambda bi, i: (0, 0)
    return pl.pallas_call(
        _qkv_kernel,
        grid=(b, s // rows),
        in_specs=[
            pl.BlockSpec((1, rows, d), lambda bi, i: (bi, i, 0)),
            pl.BlockSpec((1, d), const),
            pl.BlockSpec((d, d), const),
            pl.BlockSpec((d, d), const),
            pl.BlockSpec((d, d), const),
        ],
        out_specs=[
            pl.BlockSpec((1, d, rows), lambda bi, i: (bi, 0, i)),
            pl.BlockSpec((1, rows, d), lambda bi, i: (bi, i, 0)),
            pl.BlockSpec((1, ATTN_HEADS * V_AUG, rows), lambda bi, i: (bi, 0, i)),
        ],
        out_shape=[
            jax.ShapeDtypeStruct((b, d, s), BF16),
            jax.ShapeDtypeStruct((b, s, d), BF16),
            jax.ShapeDtypeStruct((b, ATTN_HEADS * V_AUG, s), BF16),
        ],
        compiler_params=pltpu.CompilerParams(
            dimension_semantics=("parallel", "parallel"),
            vmem_limit_bytes=VMEM_LIMIT_BYTES),
        name="qkv_proj",
    )(x, gain, wqt, wk, wvt)


def _attn_kernel(shift_ref, qt_ref, qt_next_ref, k_ref, vt_ref, bias_ref, lq1_ref, lk1_ref,
                 lq2_ref, lk2_ref, subln_ref, o_ref, w_ref, w_next_ref, acc_ref, s_a, s_b,
                 p_a, p_b, kmax_ref, first_block_ref):
    tq, tk = ATTN_TQ, ATTN_TK
    buf_a = (s_a, p_a)
    buf_b = (s_b, p_b)
    head = pl.program_id(1)
    qi = pl.program_id(2)
    shift = shift_ref[0, head]
    inv_slope = shift_ref[1, head]

    @pl.when(qi == 0)
    def _():
        dim = lax.broadcasted_iota(jnp.int32, (V_DIM, V_DIM), 0)
        col = lax.broadcasted_iota(jnp.int32, (V_DIM, V_DIM), 1)
        map_of_dim = jnp.where(dim >= HEAD_DIM, 1, 0)
        select = jnp.where(col == map_of_dim, 1.0, 0.0).astype(BF16)

        def chunk_max(c, best):
            kk = k_ref[0, pl.ds(pl.multiple_of(c * KNORM_ROWS, KNORM_ROWS), KNORM_ROWS), :]
            norms = jnp.dot(kk * kk, select, preferred_element_type=F32)
            return jnp.maximum(best, jnp.max(norms, axis=0, keepdims=True))

        kmax_ref[...] = lax.fori_loop(0, k_ref.shape[1] // KNORM_ROWS, chunk_max,
                                      jnp.zeros((1, V_DIM), F32))

    def scores(j, buf, weights=w_ref):
        start = pl.multiple_of(j * tk, tk)
        k_aug = jnp.concatenate([k_ref[0, pl.ds(start, tk), :], bias_ref[0]], axis=1)
        buf[0][...] = jnp.dot(k_aug, weights[...], preferred_element_type=F32)

    def open_tile(tile, qt, weights):
        qf = qt.astype(F32)
        ktf = k_ref[0, pl.ds(pl.multiple_of(tile * tq, tq), tq), :].astype(F32).T
        bound = None
        for mi in range(2):
            rows = slice(mi * HEAD_DIM, (mi + 1) * HEAD_DIM)
            q_sq = jnp.max(jnp.sum(qf[rows] * qf[rows], axis=0, keepdims=True), axis=1,
                           keepdims=True)
            self_min = jnp.min(jnp.sum(qf[rows] * ktf[rows], axis=0, keepdims=True), axis=1,
                               keepdims=True)
            k_sq = kmax_ref[:, mi:mi + 1]
            w_map = NORM_SLACK * jnp.sqrt(q_sq * k_sq) - self_min + SKIP_MARGIN * LOG2E
            bound = w_map if bound is None else jnp.maximum(bound, w_map)
        first_query = jnp.full((1, 1), tile * tq, jnp.int32).astype(F32)
        skip_f = jnp.floor((first_query - bound * inv_slope) * (1.0 / tk))
        first_block = jnp.clip(jnp.clip(skip_f, 0.0, float(2 ** 20)).astype(jnp.int32)[0, 0], 0,
                               tile * (tq // tk))
        zeros = jnp.zeros((HEAD_DIM, tq), BF16)
        row = lax.broadcasted_iota(jnp.int32, (V_DIM, tq), 0)
        ones = jnp.where(row < BIAS_TERMS, 1.0, 0.0).astype(BF16)
        weights[:, 0:tq] = jnp.concatenate([qt[0:HEAD_DIM], zeros, ones], axis=0)
        weights[:, tq:2 * tq] = jnp.concatenate([zeros, qt[HEAD_DIM:V_DIM], ones], axis=0)
        first_block_ref[0] = first_block
        scores(first_block, buf_a, weights)

    @pl.when(qi == 0)
    def _():
        open_tile(qi, qt_ref[0], w_ref)

    @pl.when(qi > 0)
    def _():
        w_ref[...] = w_next_ref[...]

    j0 = first_block_ref[0]
    acc_ref[...] = jnp.zeros(acc_ref.shape, F32)
    p_b[...] = jnp.zeros(p_b.shape, BF16)


    def softmax(buf, diag, m, token):
        s_buf, p_buf = buf
        m_out, alpha_out = [], []
        for mi in range(2):
            m_strips, alpha_strips = [], []
            for q0 in range(0, tq, SOFTMAX_LANES):
                strip = slice(q0, q0 + SOFTMAX_LANES)
                sm = s_buf[:, mi * tq + q0:mi * tq + q0 + SOFTMAX_LANES]
                if diag is not None:
                    key_pos = lax.broadcasted_iota(jnp.int32, sm.shape, 0) + diag * tk
                    qry_pos = lax.broadcasted_iota(jnp.int32, sm.shape, 1) + q0
                    sm = jnp.where(key_pos <= qry_pos, sm, MASK_VALUE)
                sm3 = sm.reshape(tk // SUBLANES, SUBLANES, SOFTMAX_LANES)
                blk_max = jnp.max(sm3, axis=0)
                for rot in (4, 2, 1):
                    blk_max = jnp.maximum(blk_max, pltpu.roll(blk_max, rot, 0))
                m_prev = m[mi][:, strip] - shift
                m_new = jnp.maximum(m_prev, blk_max)
                alpha_strips.append(jnp.exp2(m_prev - m_new))
                m_strips.append(m_new)
                arg = (sm3 - (m_new + 0.0 * token[mi][:, strip])[None]).reshape(tk, SOFTMAX_LANES)
                p_buf[mi, :, strip] = jnp.exp2(arg.astype(BF16))
            alpha_out.append(jnp.concatenate(alpha_strips, axis=1))
            m_out.append(jnp.concatenate(m_strips, axis=1))
        return tuple(m_out), tuple(alpha_out)

    def pv_update(j, buf, alpha):
        _, p_buf = buf
        start = pl.multiple_of(j * tk, tk)
        vt = vt_ref[0, :, pl.ds(start, tk)]
        token = []
        for mi in range(2):
            pv = jnp.dot(vt, p_buf[mi], preferred_element_type=F32)
            acc3 = acc_ref[mi].reshape(V_AUG // SUBLANES, SUBLANES, tq)
            acc_ref[mi] = (alpha[mi][None] * acc3).reshape(V_AUG, tq) + pv
            token.append(pv[V_DIM:V_DIM + SUBLANES])
        return tuple(token)

    def step(j, cur, nxt, state, diag=None, prefetch=True):
        m, alpha_prev, token_cur = state
        if prefetch:
            scores(j + 1, nxt)
        token_nxt = pv_update(jnp.maximum(j - 1, 0), nxt, alpha_prev)
        m, alpha = softmax(cur, diag, m, token_cur)
        return m, alpha, token_nxt

    def finish(cur, nxt, state):
        for d in range(n_diag):
            state = step(first_diag + d, cur, nxt, state, diag=d, prefetch=d + 1 < n_diag)
            cur, nxt = nxt, cur
        open_tile(jnp.minimum(qi + 1, pl.num_programs(2) - 1), qt_next_ref[0], w_next_ref)
        pv_update(first_diag + n_diag - 1, nxt, state[1])
        write_output()

    def write_output():
        lam = (jnp.exp(jnp.sum(lq1_ref[...] * lk1_ref[...], keepdims=True))
               - jnp.exp(jnp.sum(lq2_ref[...] * lk2_ref[...], keepdims=True)) + LAM_INIT)
        o1 = acc_ref[0, 0:V_DIM, :] * (1.0 / acc_ref[0, V_DIM:V_DIM + 1, :])
        o2 = acc_ref[1, 0:V_DIM, :] * (1.0 / acc_ref[1, V_DIM:V_DIM + 1, :])
        ot = o1 - lam * o2
        inv_rms = lax.rsqrt(jnp.mean(ot * ot, axis=0, keepdims=True) + NORM_EPS)
        gain = jnp.tile(subln_ref[...], (1, tq // V_DIM))
        o_ref[0] = (ot * inv_rms * gain * (1.0 - LAM_INIT)).T.astype(BF16)

    n_diag = tq // tk
    first_diag = qi * n_diag
    n_full = first_diag - j0

    def two_steps(t, state):
        state = step(j0 + 2 * t, buf_a, buf_b, state)
        return step(j0 + 2 * t + 1, buf_b, buf_a, state)

    row_init = lambda v: (jnp.full((SUBLANES, tq), v, F32), jnp.full((SUBLANES, tq), v, F32))
    state = lax.fori_loop(0, n_full // 2, two_steps,
                          (row_init(MASK_VALUE), row_init(1.0), row_init(0.0)))

    @pl.when(n_full % 2 == 1)
    def _():
        finish(buf_b, buf_a, step(first_diag - 1, buf_a, buf_b, state))

    @pl.when(n_full % 2 == 0)
    def _():
        finish(buf_a, buf_b, state)


def _alibi_tables():
    slopes = np.asarray([2.0 ** (-8.0 * (h + 1) / ATTN_HEADS) for h in range(ATTN_HEADS)]) * LOG2E
    rest = slopes[:, None] * np.arange(ATTN_TK, dtype=np.float64)[None, :]
    tile = np.zeros((ATTN_HEADS, ATTN_TK, V_DIM), np.float32)
    for term in range(BIAS_TERMS):
        part = rest.astype(jnp.bfloat16)
        tile[:, :, term] = part.astype(np.float32)
        rest = rest - part.astype(np.float64)
    scalars = np.stack([slopes * ATTN_TK, 1.0 / slopes]).astype(np.float32)
    return jnp.asarray(tile, BF16), jnp.asarray(scalars)


def _attn_call(qt, k, vt, lq1, lk1, lq2, lk2, subln):
    b, s, d = k.shape
    tq, tk = ATTN_TQ, ATTN_TK
    bias, shifts = _alibi_tables()
    vec = lambda n: pl.BlockSpec((1, n), lambda bi, h, i: (0, 0))
    return pl.pallas_call(
        _attn_kernel,
        grid=(b, ATTN_HEADS, s // tq),
        in_specs=[
            pl.BlockSpec(memory_space=pltpu.SMEM),
            pl.BlockSpec((1, V_DIM, tq), lambda bi, h, i: (bi, h, i)),
            pl.BlockSpec((1, V_DIM, tq), lambda bi, h, i: (bi, h, jnp.minimum(i + 1, s // tq - 1))),
            pl.BlockSpec((1, s, V_DIM), lambda bi, h, i: (bi, 0, h)),
            pl.BlockSpec((1, V_AUG, s), lambda bi, h, i: (bi, h, 0)),
            pl.BlockSpec((1, tk, V_DIM), lambda bi, h, i: (h, 0, 0)),
            vec(HEAD_DIM), vec(HEAD_DIM), vec(HEAD_DIM), vec(HEAD_DIM),
            pl.BlockSpec((V_DIM, V_DIM), lambda bi, h, i: (0, 0)),
        ],
        out_specs=pl.BlockSpec((1, tq, V_DIM), lambda bi, h, i: (bi, i, h)),
        out_shape=jax.ShapeDtypeStruct((b, s, d), BF16),
        scratch_shapes=[
            pltpu.VMEM((2 * V_DIM, 2 * tq), BF16),
            pltpu.VMEM((2 * V_DIM, 2 * tq), BF16),
            pltpu.VMEM((2, V_AUG, tq), F32),
            pltpu.VMEM((tk, 2 * tq), F32),
            pltpu.VMEM((tk, 2 * tq), F32),
            pltpu.VMEM((2, tk, tq), BF16),
            pltpu.VMEM((2, tk, tq), BF16),
            pltpu.VMEM((1, V_DIM), F32),
            pltpu.SMEM((1,), jnp.int32),
        ],
        compiler_params=pltpu.CompilerParams(
            dimension_semantics=("parallel", "parallel", "arbitrary"),
            vmem_limit_bytes=VMEM_LIMIT_BYTES),
        name="diff_attn",
    )(shifts, qt, qt, k, vt, bias, lq1, lk1, lq2, lk2, subln)


def _ffn_kernel(*refs, tiles_per_seq, with_mixer_proj):
    if with_mixer_proj:
        mix_ref, wproj_ref, gmix_ref, *refs = refs
    (x_ref, gpre_ref, wa_ref, wg_ref, cw_ref, wd_ref, gpost_ref, o_ref,
     hn_ref, abuf_ref, carry_ref, act_ref) = refs
    rows = x_ref.shape[0]
    seq_start = (pl.program_id(0) % tiles_per_seq) == 0
    x = x_ref[...]
    if with_mixer_proj:
        proj = jnp.dot(mix_ref[...], wproj_ref[...], preferred_element_type=F32)
        x = x + _rms(proj, gmix_ref[...])
    hn_ref[...] = _rms(x, gpre_ref[...]).astype(BF16)
    for c in range(FFN_DIM // FFN_CHUNK):
        cols = slice(c * FFN_CHUNK, (c + 1) * FFN_CHUNK)
        a = jnp.dot(hn_ref[...], wa_ref[:, cols], preferred_element_type=F32)
        abuf_ref[0:SUBLANES, :] = jnp.where(seq_start, 0.0, carry_ref[c])
        abuf_ref[SUBLANES:SUBLANES + rows, :] = a
        carry_ref[c] = a[rows - SUBLANES:rows, :]
        cw = cw_ref[:, cols]
        a_conv = (cw[3:4] + abuf_ref[SUBLANES - 2:SUBLANES - 2 + rows, :] * cw[0:1]
                  + abuf_ref[SUBLANES - 1:SUBLANES - 1 + rows, :] * cw[1:2] + a * cw[2:3])
        g = jnp.dot(hn_ref[...], wg_ref[:, cols], preferred_element_type=F32)
        act_ref[:, cols] = (_gelu_tanh(a_conv) * g).astype(BF16)
    f = jnp.dot(act_ref[...], wd_ref[...], preferred_element_type=F32)
    o_ref[...] = x + _rms(f, gpost_ref[...])


def _ffn_call(x2d, seq_len, gpre, w_up, conv_tab, w_down, gpost, mixer_proj=None):
    t, d = x2d.shape
    rows = ROW_TILE if mixer_proj is None else ROW_TILE_WITH_PROJ
    const = lambda i: (0, 0)
    resident = pl.Buffered(1)
    mixer_specs = [] if mixer_proj is None else [
        pl.BlockSpec((rows, d), lambda i: (i, 0)),
        pl.BlockSpec((d, d), const, pipeline_mode=resident),
        pl.BlockSpec((1, d), const),
    ]
    return pl.pallas_call(
        functools.partial(_ffn_kernel, tiles_per_seq=seq_len // rows,
                          with_mixer_proj=mixer_proj is not None),
        grid=(t // rows,),
        in_specs=mixer_specs + [
            pl.BlockSpec((rows, d), lambda i: (i, 0)),
            pl.BlockSpec((1, d), const),
            pl.BlockSpec((d, FFN_DIM), lambda i: (0, 0), pipeline_mode=resident),
            pl.BlockSpec((d, FFN_DIM), lambda i: (0, 1), pipeline_mode=resident),
            pl.BlockSpec((SUBLANES, FFN_DIM), const),
            pl.BlockSpec((FFN_DIM, d), const, pipeline_mode=resident),
            pl.BlockSpec((1, d), const),
        ],
        out_specs=pl.BlockSpec((rows, d), lambda i: (i, 0)),
        out_shape=jax.ShapeDtypeStruct((t, d), F32),
        scratch_shapes=[
            pltpu.VMEM((rows, d), BF16),
            pltpu.VMEM((rows + SUBLANES, FFN_CHUNK), F32),
            pltpu.VMEM((FFN_DIM // FFN_CHUNK, SUBLANES, FFN_CHUNK), F32),
            pltpu.VMEM((rows, FFN_DIM), BF16),
        ],
        compiler_params=pltpu.CompilerParams(
            dimension_semantics=("arbitrary",),
            vmem_limit_bytes=VMEM_LIMIT_BYTES),
        name="conv_ffn",
    )(*(mixer_proj or ()), x2d, gpre, w_up, w_up, conv_tab, w_down, gpost)


def _sgu_kernel(x_ref, gpre_ref, win_ref, lng_ref, lnb_ref, ws_ref, bst_ref, wout_ref,
                gpost_ref, o_ref, hn_ref, v_ref, vn_ref, y_ref):
    rows = x_ref.shape[0]
    gd = SGU_GROUP_DIM
    x = x_ref[...]
    hn_ref[...] = _rms(x, gpre_ref[...]).astype(BF16)
    for g in range(SGU_GROUPS):
        cols = slice(SGU_WIDTH + g * gd, SGU_WIDTH + (g + 1) * gd)
        v_ref[:, g * gd:(g + 1) * gd] = _gelu_tanh(
            jnp.dot(hn_ref[...], win_ref[:, cols], preferred_element_type=F32))
    v = v_ref[...]
    mu = jnp.mean(v, axis=-1, keepdims=True)
    vc = v - mu
    var = jnp.mean(vc * vc, axis=-1, keepdims=True)
    vn_ref[...] = (vc * lax.rsqrt(var + LN_EPS) * lng_ref[...] + lnb_ref[...]).astype(BF16)
    t_pos = lax.broadcasted_iota(jnp.int32, (CHUNK, CHUNK), 0)
    s_pos = lax.broadcasted_iota(jnp.int32, (CHUNK, CHUNK), 1)
    for g in range(SGU_GROUPS):
        cols = slice(g * gd, (g + 1) * gd)
        u = _gelu_tanh(jnp.dot(hn_ref[...], win_ref[:, cols], preferred_element_type=F32))
        w = jnp.where(s_pos <= t_pos, ws_ref[g], 0.0).astype(BF16)
        bias = bst_ref[:, g:g + 1]
        for c in range(rows // CHUNK):
            r = slice(c * CHUNK, (c + 1) * CHUNK)
            s = jnp.dot(w, vn_ref[r, cols], preferred_element_type=F32) + bias
            y_ref[r, cols] = (u[r] * s).astype(BF16)
    out = jnp.dot(y_ref[...], wout_ref[...], preferred_element_type=F32)
    o_ref[...] = x + _rms(out, gpost_ref[...])


def _sgu_call(x2d, gpre, w_in, ln_g, ln_b, w_s, b_st, w_out, gpost):
    t, d = x2d.shape
    rows = ROW_TILE
    const = lambda i: (0, 0)
    resident = pl.Buffered(1)
    return pl.pallas_call(
        _sgu_kernel,
        grid=(t // rows,),
        in_specs=[
            pl.BlockSpec((rows, d), lambda i: (i, 0)),
            pl.BlockSpec((1, d), const),
            pl.BlockSpec((d, 2 * SGU_WIDTH), const, pipeline_mode=resident),
            pl.BlockSpec((1, SGU_WIDTH), const),
            pl.BlockSpec((1, SGU_WIDTH), const),
            pl.BlockSpec((SGU_GROUPS, CHUNK, CHUNK), lambda i: (0, 0, 0)),
            pl.BlockSpec((CHUNK, SGU_GROUPS), const),
            pl.BlockSpec((SGU_WIDTH, d), const, pipeline_mode=resident),
            pl.BlockSpec((1, d), const),
        ],
        out_specs=pl.BlockSpec((rows, d), lambda i: (i, 0)),
        out_shape=jax.ShapeDtypeStruct((t, d), F32),
        scratch_shapes=[
            pltpu.VMEM((rows, d), BF16),
            pltpu.VMEM((rows, SGU_WIDTH), F32),
            pltpu.VMEM((rows, SGU_WIDTH), BF16),
            pltpu.VMEM((rows, SGU_WIDTH), BF16),
        ],
        compiler_params=pltpu.CompilerParams(
            dimension_semantics=("parallel",),
            vmem_limit_bytes=VMEM_LIMIT_BYTES),
        name="chunked_sgu",
    )(x2d, gpre, w_in, ln_g, ln_b, w_s, b_st, w_out, gpost)


def _conv_table(conv_w, conv_b):
    pad = jnp.zeros((SUBLANES - 4, FFN_DIM), F32)
    return jnp.concatenate([conv_w, conv_b[None, :], pad], axis=0)


def _row(v):
    return v.reshape(1, -1)


def kernel(x, attn_w_qkv, attn_lambda_q1, attn_lambda_k1, attn_lambda_q2, attn_lambda_k2, attn_subln, attn_w_o, sgu_w_in, sgu_ln_g, sgu_ln_b, sgu_w_s, sgu_b_s, sgu_w_out, norm_mix_pre, norm_mix_post, norm_ffn_pre, norm_ffn_post, ffn_w_up, ffn_conv_w, ffn_conv_b, ffn_w_down):
    b, s, d = x.shape
    assert d == D_MODEL and s % max(QKV_ROWS, ROW_TILE, ATTN_TQ) == 0 and ATTN_TQ % ATTN_TK == 0

    w_qkv = attn_w_qkv[0]
    wqt = (w_qkv[:, :d].T * (HEAD_DIM ** -0.5 * LOG2E)).astype(BF16)
    wk = w_qkv[:, d:2 * d].astype(BF16)
    wvt = w_qkv[:, 2 * d:].T.astype(BF16)
    qt, k, vt = _qkv_call(x, _row(norm_mix_pre[0]), wqt, wk, wvt)
    attn = _attn_call(qt, k, vt, _row(attn_lambda_q1[0]), _row(attn_lambda_k1[0]),
                      _row(attn_lambda_q2[0]), _row(attn_lambda_k2[0]),
                      jnp.broadcast_to(attn_subln[0][:, None], (V_DIM, V_DIM)))
    x2d = x.reshape(b * s, d)
    x2d = _ffn_call(x2d, s, _row(norm_ffn_pre[0]), ffn_w_up[0].astype(BF16),
                    _conv_table(ffn_conv_w[0], ffn_conv_b[0]), ffn_w_down[0].astype(BF16),
                    _row(norm_ffn_post[0]),
                    mixer_proj=(attn.reshape(b * s, d), attn_w_o[0].astype(BF16),
                                _row(norm_mix_post[0])))

    x2d = _sgu_call(x2d, _row(norm_mix_pre[1]), sgu_w_in[0].astype(BF16), _row(sgu_ln_g[0]),
                    _row(sgu_ln_b[0]), sgu_w_s[0], sgu_b_s[0].T, sgu_w_out[0].astype(BF16),
                    _row(norm_mix_post[1]))
    x2d = _ffn_call(x2d, s, _row(norm_ffn_pre[1]), ffn_w_up[1].astype(BF16),
                    _conv_table(ffn_conv_w[1], ffn_conv_b[1]), ffn_w_down[1].astype(BF16),
                    _row(norm_ffn_post[1]))
    return x2d.reshape(b, s, d)
```

```python
import functools
import math

import jax
import jax.numpy as jnp
import numpy as np
from jax import lax
from jax.experimental import pallas as pl
from jax.experimental.pallas import tpu as pltpu

F32 = jnp.float32
BF16 = jnp.bfloat16

D_MODEL = 1024
ATTN_HEADS = 8
HEAD_DIM = 64
V_DIM = 2 * HEAD_DIM
SGU_WIDTH = 2 * D_MODEL
SGU_GROUPS = 8
SGU_GROUP_DIM = SGU_WIDTH // SGU_GROUPS
CHUNK = 128
FFN_DIM = 2816
NORM_EPS = 1e-6
LN_EPS = 1e-5
LAM_INIT = 0.8 - 0.6 * math.exp(-0.3 * 0)

MXU_DIM = 256
SUBLANES = 8
V_PAD = 16
V_AUG = V_DIM + V_PAD
LOG2E = math.log2(math.e)
BIAS_TERMS = 4
VMEM_LIMIT_BYTES = 56 * 1024 * 1024

QKV_ROWS = 512
ATTN_TQ = 512
ATTN_TK = 512
ROW_TILE = 1024
ROW_TILE_WITH_PROJ = 512
FFN_CHUNK = MXU_DIM
MASK_VALUE = -1e30
KNORM_ROWS = 1024
SKIP_MARGIN = 110.0
NORM_SLACK = 1.02

_NT_DIMS = (((1,), (1,)), ((), ()))


def _rms(x, gain):
    return x * lax.rsqrt(jnp.mean(x * x, axis=-1, keepdims=True) + NORM_EPS) * gain


def _gelu_tanh(x):
    return x * (0.5 * (1.0 + jnp.tanh(0.7978845608028654 * (x + 0.044715 * (x * x * x)))))


def _qkv_kernel(x_ref, g_ref, wqt_ref, wk_ref, wvt_ref, qt_ref, k_ref, vt_ref):
    rows = x_ref.shape[1]
    hn = _rms(x_ref[0], g_ref[...]).astype(BF16)
    k_ref[0] = jnp.dot(hn, wk_ref[...], preferred_element_type=F32).astype(BF16)
    qt_ref[0] = lax.dot_general(wqt_ref[...], hn, _NT_DIMS,
                                preferred_element_type=F32).astype(BF16)
    vt = lax.dot_general(wvt_ref[...], hn, _NT_DIMS,
                         preferred_element_type=F32).astype(BF16)
    pad_row = lax.broadcasted_iota(jnp.int32, (V_PAD, rows), 0)
    ones_rows = jnp.where(pad_row == 0, 1.0, 0.0).astype(BF16)
    for h in range(ATTN_HEADS):
        vt_ref[0, h * V_AUG:h * V_AUG + V_DIM, :] = vt[h * V_DIM:(h + 1) * V_DIM]
        vt_ref[0, h * V_AUG + V_DIM:(h + 1) * V_AUG, :] = ones_rows


def _qkv_call(x, gain, wqt, wk, wvt):
    b, s, d = x.shape
    rows = QKV_ROWS
    const = lambda bi, i: (0, 0)
    return pl.pallas_call(
        _qkv_kernel,
        grid=(b, s // rows),
        in_specs=[
            pl.BlockSpec((1, rows, d), lambda bi, i: (bi, i, 0)),
            pl.BlockSpec((1, d), const),
            pl.BlockSpec((d, d), const),
            pl.BlockSpec((d, d), const),
            pl.BlockSpec((d, d), const),
        ],
        out_specs=[
            pl.BlockSpec((1, d, rows), lambda bi, i: (bi, 0, i)),
            pl.BlockSpec((1, rows, d), lambda bi, i: (bi, i, 0)),
            pl.BlockSpec((1, ATTN_HEADS * V_AUG, rows), lambda bi, i: (bi, 0, i)),
        ],
        out_shape=[
            jax.ShapeDtypeStruct((b, d, s), BF16),
            jax.ShapeDtypeStruct((b, s, d), BF16),
            jax.ShapeDtypeStruct((b, ATTN_HEADS * V_AUG, s), BF16),
        ],
        compiler_params=pltpu.CompilerParams(
            dimension_semantics=("parallel", "parallel"),
            vmem_limit_bytes=VMEM_LIMIT_BYTES),
        name="qkv_proj",
    )(x, gain, wqt, wk, wvt)


def _attn_kernel(shift_ref, qt_ref, qt_next_ref, k_ref, vt_ref, bias_ref, lq1_ref, lk1_ref,
                 lq2_ref, lk2_ref, subln_ref, o_ref, w_ref, w_next_ref, acc_ref, s_a, s_b,
                 p_a, p_b, kmax_ref, first_block_ref):
    tq, tk = ATTN_TQ, ATTN_TK
    buf_a = (s_a, p_a)
    buf_b = (s_b, p_b)
    head = pl.program_id(1)
    qi = pl.program_id(2)
    shift = shift_ref[0, head]
    inv_slope = shift_ref[1, head]

    @pl.when(qi == 0)
    def _():
        dim = lax.broadcasted_iota(jnp.int32, (V_DIM, V_DIM), 0)
        col = lax.broadcasted_iota(jnp.int32, (V_DIM, V_DIM), 1)
        map_of_dim = jnp.where(dim >= HEAD_DIM, 1, 0)
        select = jnp.where(col == map_of_dim, 1.0, 0.0).astype(BF16)

        def chunk_max(c, best):
            kk = k_ref[0, pl.ds(pl.multiple_of(c * KNORM_ROWS, KNORM_ROWS), KNORM_ROWS), :]
            norms = jnp.dot(kk * kk, select, preferred_element_type=F32)
            return jnp.maximum(best, jnp.max(norms, axis=0, keepdims=True))

        kmax_ref[...] = lax.fori_loop(0, k_ref.shape[1] // KNORM_ROWS, chunk_max,
                                      jnp.zeros((1, V_DIM), F32))

    def scores(j, buf, weights=w_ref):
        start = pl.multiple_of(j * tk, tk)
        k_aug = jnp.concatenate([k_ref[0, pl.ds(start, tk), :], bias_ref[0]], axis=1)
        buf[0][...] = jnp.dot(k_aug, weights[...], preferred_element_type=F32)

    def open_tile(tile, qt, weights):
        qf = qt.astype(F32)
        ktf = k_ref[0, pl.ds(pl.multiple_of(tile * tq, tq), tq), :].astype(F32).T
        bound = None
        for mi in range(2):
            rows = slice(mi * HEAD_DIM, (mi + 1) * HEAD_DIM)
            q_sq = jnp.max(jnp.sum(qf[rows] * qf[rows], axis=0, keepdims=True), axis=1,
                           keepdims=True)
            self_min = jnp.min(jnp.sum(qf[rows] * ktf[rows], axis=0, keepdims=True), axis=1,
                               keepdims=True)
            k_sq = kmax_ref[:, mi:mi + 1]
            w_map = NORM_SLACK * jnp.sqrt(q_sq * k_sq) - self_min + SKIP_MARGIN * LOG2E
            bound = w_map if bound is None else jnp.maximum(bound, w_map)
        first_query = jnp.full((1, 1), tile * tq, jnp.int32).astype(F32)
        skip_f = jnp.floor((first_query - bound * inv_slope) * (1.0 / tk))
        first_block = jnp.clip(jnp.clip(skip_f, 0.0, float(2 ** 20)).astype(jnp.int32)[0, 0], 0,
                               tile * (tq // tk))
        zeros = jnp.zeros((HEAD_DIM, tq), BF16)
        row = lax.broadcasted_iota(jnp.int32, (V_DIM, tq), 0)
        ones = jnp.where(row < BIAS_TERMS, 1.0, 0.0).astype(BF16)
        weights[:, 0:tq] = jnp.concatenate([qt[0:HEAD_DIM], zeros, ones], axis=0)
        weights[:, tq:2 * tq] = jnp.concatenate([zeros, qt[HEAD_DIM:V_DIM], ones], axis=0)
        first_block_ref[0] = first_block
        scores(first_block, buf_a, weights)

    @pl.when(qi == 0)
    def _():
        open_tile(qi, qt_ref[0], w_ref)

    @pl.when(qi > 0)
    def _():
        w_ref[...] = w_next_ref[...]

    j0 = first_block_ref[0]
    acc_ref[...] = jnp.zeros(acc_ref.shape, F32)
    p_b[...] = jnp.zeros(p_b.shape, BF16)


    def softmax(buf, diag, m, token):
        s_buf, p_buf = buf
        m_out, alpha_out = [], []
        for mi in range(2):
            sm = s_buf[:, mi * tq:(mi + 1) * tq]
            if diag is not None:
                key_pos = lax.broadcasted_iota(jnp.int32, (tk, tq), 0) + diag * tk
                qry_pos = lax.broadcasted_iota(jnp.int32, (tk, tq), 1)
                sm = jnp.where(key_pos <= qry_pos, sm, MASK_VALUE)
            sm3 = sm.reshape(tk // SUBLANES, SUBLANES, tq)
            blk_max = jnp.max(sm3, axis=0)
            for rot in (4, 2, 1):
                blk_max = jnp.maximum(blk_max, pltpu.roll(blk_max, rot, 0))
            m_prev = m[mi] - shift
            m_new = jnp.maximum(m_prev, blk_max)
            alpha_out.append(jnp.exp2(m_prev - m_new))
            m_out.append(m_new)
            arg = (sm3 - (m_new + 0.0 * token[mi])[None]).reshape(tk, tq)
            p_buf[mi] = jnp.exp2(arg.astype(BF16))
        return tuple(m_out), tuple(alpha_out)

    def pv_update(j, buf, alpha):
        _, p_buf = buf
        start = pl.multiple_of(j * tk, tk)
        token = []
        for mi in range(2):
            vt = vt_ref[0, :, pl.ds(start, tk)]
            pv = jnp.dot(vt, p_buf[mi], preferred_element_type=F32)
            acc3 = acc_ref[mi].reshape(V_AUG // SUBLANES, SUBLANES, tq)
            acc_ref[mi] = (alpha[mi][None] * acc3).reshape(V_AUG, tq) + pv
            token.append(pv[V_DIM:V_DIM + SUBLANES])
        return tuple(token)

    def step(j, cur, nxt, state, diag=None, prefetch=True):
        m, alpha_prev, token_cur = state
        if prefetch:
            scores(j + 1, nxt)
        token_nxt = pv_update(jnp.maximum(j - 1, 0), nxt, alpha_prev)
        m, alpha = softmax(cur, diag, m, token_cur)
        return m, alpha, token_nxt

    def finish(cur, nxt, state):
        for d in range(n_diag):
            state = step(first_diag + d, cur, nxt, state, diag=d, prefetch=d + 1 < n_diag)
            cur, nxt = nxt, cur
        open_tile(jnp.minimum(qi + 1, pl.num_programs(2) - 1), qt_next_ref[0], w_next_ref)
        pv_update(first_diag + n_diag - 1, nxt, state[1])
        write_output()

    def write_output():
        lam = (jnp.exp(jnp.sum(lq1_ref[...] * lk1_ref[...], keepdims=True))
               - jnp.exp(jnp.sum(lq2_ref[...] * lk2_ref[...], keepdims=True)) + LAM_INIT)
        o1 = acc_ref[0, 0:V_DIM, :] * (1.0 / acc_ref[0, V_DIM:V_DIM + 1, :])
        o2 = acc_ref[1, 0:V_DIM, :] * (1.0 / acc_ref[1, V_DIM:V_DIM + 1, :])
        ot = o1 - lam * o2
        inv_rms = lax.rsqrt(jnp.mean(ot * ot, axis=0, keepdims=True) + NORM_EPS)
        gain = jnp.tile(subln_ref[...], (1, tq // V_DIM))
        o_ref[0] = (ot * inv_rms * gain * (1.0 - LAM_INIT)).T.astype(BF16)

    n_diag = tq // tk
    first_diag = qi * n_diag
    n_full = first_diag - j0

    def two_steps(t, state):
        state = step(j0 + 2 * t, buf_a, buf_b, state)
        return step(j0 + 2 * t + 1, buf_b, buf_a, state)

    row_init = lambda v: (jnp.full((SUBLANES, tq), v, F32), jnp.full((SUBLANES, tq), v, F32))
    state = lax.fori_loop(0, n_full // 2, two_steps,
                          (row_init(MASK_VALUE), row_init(1.0), row_init(0.0)))

    @pl.when(n_full % 2 == 1)
    def _():
        finish(buf_b, buf_a, step(first_diag - 1, buf_a, buf_b, state))

    @pl.when(n_full % 2 == 0)
    def _():
        finish(buf_a, buf_b, state)


def _alibi_tables():
    slopes = np.asarray([2.0 ** (-8.0 * (h + 1) / ATTN_HEADS) for h in range(ATTN_HEADS)]) * LOG2E
    rest = slopes[:, None] * np.arange(ATTN_TK, dtype=np.float64)[None, :]
    tile = np.zeros((ATTN_HEADS, ATTN_TK, V_DIM), np.float32)
    for term in range(BIAS_TERMS):
        part = rest.astype(jnp.bfloat16)
        tile[:, :, term] = part.astype(np.float32)
        rest = rest - part.astype(np.float64)
    scalars = np.stack([slopes * ATTN_TK, 1.0 / slopes]).astype(np.float32)
    return jnp.asarray(tile, BF16), jnp.asarray(scalars)


def _attn_call(qt, k, vt, lq1, lk1, lq2, lk2, subln):
    b, s, d = k.shape
    tq, tk = ATTN_TQ, ATTN_TK
    bias, shifts = _alibi_tables()
    vec = lambda n: pl.BlockSpec((1, n), lambda bi, h, i: (0, 0))
    return pl.pallas_call(
        _attn_kernel,
        grid=(b, ATTN_HEADS, s // tq),
        in_specs=[
            pl.BlockSpec(memory_space=pltpu.SMEM),
            pl.BlockSpec((1, V_DIM, tq), lambda bi, h, i: (bi, h, i)),
            pl.BlockSpec((1, V_DIM, tq), lambda bi, h, i: (bi, h, jnp.minimum(i + 1, s // tq - 1))),
            pl.BlockSpec((1, s, V_DIM), lambda bi, h, i: (bi, 0, h)),
            pl.BlockSpec((1, V_AUG, s), lambda bi, h, i: (bi, h, 0)),
            pl.BlockSpec((1, tk, V_DIM), lambda bi, h, i: (h, 0, 0)),
            vec(HEAD_DIM), vec(HEAD_DIM), vec(HEAD_DIM), vec(HEAD_DIM),
            pl.BlockSpec((V_DIM, V_DIM), lambda bi, h, i: (0, 0)),
        ],
        out_specs=pl.BlockSpec((1, tq, V_DIM), lambda bi, h, i: (bi, i, h)),
        out_shape=jax.ShapeDtypeStruct((b, s, d), BF16),
        scratch_shapes=[
            pltpu.VMEM((2 * V_DIM, 2 * tq), BF16),
            pltpu.VMEM((2 * V_DIM, 2 * tq), BF16),
            pltpu.VMEM((2, V_AUG, tq), F32),
            pltpu.VMEM((tk, 2 * tq), F32),
            pltpu.VMEM((tk, 2 * tq), F32),
            pltpu.VMEM((2, tk, tq), BF16),
            pltpu.VMEM((2, tk, tq), BF16),
            pltpu.VMEM((1, V_DIM), F32),
            pltpu.SMEM((1,), jnp.int32),
        ],
        compiler_params=pltpu.CompilerParams(
            dimension_semantics=("parallel", "parallel", "arbitrary"),
            vmem_limit_bytes=VMEM_LIMIT_BYTES),
        name="diff_attn",
    )(shifts, qt, qt, k, vt, bias, lq1, lk1, lq2, lk2, subln)


def _ffn_kernel(*refs, tiles_per_seq, with_mixer_proj):
    if with_mixer_proj:
        mix_ref, wproj_ref, gmix_ref, *refs = refs
    (x_ref, gpre_ref, wa_ref, wg_ref, cw_ref, wd_ref, gpost_ref, o_ref,
     hn_ref, abuf_ref, carry_ref, act_ref) = refs
    rows = x_ref.shape[0]
    seq_start = (pl.program_id(0) % tiles_per_seq) == 0
    x = x_ref[...]
    if with_mixer_proj:
        proj = jnp.dot(mix_ref[...], wproj_ref[...], preferred_element_type=F32)
        x = x + _rms(proj, gmix_ref[...])
    hn_ref[...] = _rms(x, gpre_ref[...]).astype(BF16)
    for c in range(FFN_DIM // FFN_CHUNK):
        cols = slice(c * FFN_CHUNK, (c + 1) * FFN_CHUNK)
        a = jnp.dot(hn_ref[...], wa_ref[:, cols], preferred_element_type=F32)
        abuf_ref[0:SUBLANES, :] = jnp.where(seq_start, 0.0, carry_ref[c])
        abuf_ref[SUBLANES:SUBLANES + rows, :] = a
        carry_ref[c] = a[rows - SUBLANES:rows, :]
        cw = cw_ref[:, cols]
        a_conv = (cw[3:4] + abuf_ref[SUBLANES - 2:SUBLANES - 2 + rows, :] * cw[0:1]
                  + abuf_ref[SUBLANES - 1:SUBLANES - 1 + rows, :] * cw[1:2] + a * cw[2:3])
        g = jnp.dot(hn_ref[...], wg_ref[:, cols], preferred_element_type=F32)
        act_ref[:, cols] = (_gelu_tanh(a_conv) * g).astype(BF16)
    f = jnp.dot(act_ref[...], wd_ref[...], preferred_element_type=F32)
    o_ref[...] = x + _rms(f, gpost_ref[...])


def _ffn_call(x2d, seq_len, gpre, w_up, conv_tab, w_down, gpost, mixer_proj=None):
    t, d = x2d.shape
    rows = ROW_TILE if mixer_proj is None else ROW_TILE_WITH_PROJ
    const = lambda i: (0, 0)
    resident = pl.Buffered(1)
    mixer_specs = [] if mixer_proj is None else [
        pl.BlockSpec((rows, d), lambda i: (i, 0)),
        pl.BlockSpec((d, d), const, pipeline_mode=resident),
        pl.BlockSpec((1, d), const),
    ]
    return pl.pallas_call(
        functools.partial(_ffn_kernel, tiles_per_seq=seq_len // rows,
                          with_mixer_proj=mixer_proj is not None),
        grid=(t // rows,),
        in_specs=mixer_specs + [
            pl.BlockSpec((rows, d), lambda i: (i, 0)),
            pl.BlockSpec((1, d), const),
            pl.BlockSpec((d, FFN_DIM), lambda i: (0, 0), pipeline_mode=resident),
            pl.BlockSpec((d, FFN_DIM), lambda i: (0, 1), pipeline_mode=resident),
            pl.BlockSpec((SUBLANES, FFN_DIM), const),
            pl.BlockSpec((FFN_DIM, d), const, pipeline_mode=resident),
            pl.BlockSpec((1, d), const),
        ],
        out_specs=pl.BlockSpec((rows, d), lambda i: (i, 0)),
        out_shape=jax.ShapeDtypeStruct((t, d), F32),
        scratch_shapes=[
            pltpu.VMEM((rows, d), BF16),
            pltpu.VMEM((rows + SUBLANES, FFN_CHUNK), F32),
            pltpu.VMEM((FFN_DIM // FFN_CHUNK, SUBLANES, FFN_CHUNK), F32),
            pltpu.VMEM((rows, FFN_DIM), BF16),
        ],
        compiler_params=pltpu.CompilerParams(
            dimension_semantics=("arbitrary",),
            vmem_limit_bytes=VMEM_LIMIT_BYTES),
        name="conv_ffn",
    )(*(mixer_proj or ()), x2d, gpre, w_up, w_up, conv_tab, w_down, gpost)


def _sgu_kernel(x_ref, gpre_ref, win_ref, lng_ref, lnb_ref, ws_ref, bst_ref, wout_ref,
                gpost_ref, o_ref, hn_ref, v_ref, vn_ref, y_ref):
    rows = x_ref.shape[0]
    gd = SGU_GROUP_DIM
    x = x_ref[...]
    hn_ref[...] = _rms(x, gpre_ref[...]).astype(BF16)
    for g in range(SGU_GROUPS):
        cols = slice(SGU_WIDTH + g * gd, SGU_WIDTH + (g + 1) * gd)
        v_ref[:, g * gd:(g + 1) * gd] = _gelu_tanh(
            jnp.dot(hn_ref[...], win_ref[:, cols], preferred_element_type=F32))
    v = v_ref[...]
    mu = jnp.mean(v, axis=-1, keepdims=True)
    vc = v - mu
    var = jnp.mean(vc * vc, axis=-1, keepdims=True)
    vn_ref[...] = (vc * lax.rsqrt(var + LN_EPS) * lng_ref[...] + lnb_ref[...]).astype(BF16)
    t_pos = lax.broadcasted_iota(jnp.int32, (CHUNK, CHUNK), 0)
    s_pos = lax.broadcasted_iota(jnp.int32, (CHUNK, CHUNK), 1)
    for g in range(SGU_GROUPS):
        cols = slice(g * gd, (g + 1) * gd)
        u = _gelu_tanh(jnp.dot(hn_ref[...], win_ref[:, cols], preferred_element_type=F32))
        w = jnp.where(s_pos <= t_pos, ws_ref[g], 0.0).astype(BF16)
        bias = bst_ref[:, g:g + 1]
        for c in range(rows // CHUNK):
            r = slice(c * CHUNK, (c + 1) * CHUNK)
            s = jnp.dot(w, vn_ref[r, cols], preferred_element_type=F32) + bias
            y_ref[r, cols] = (u[r] * s).astype(BF16)
    out = jnp.dot(y_ref[...], wout_ref[...], preferred_element_type=F32)
    o_ref[...] = x + _rms(out, gpost_ref[...])


def _sgu_call(x2d, gpre, w_in, ln_g, ln_b, w_s, b_st, w_out, gpost):
    t, d = x2d.shape
    rows = ROW_TILE
    const = lambda i: (0, 0)
    resident = pl.Buffered(1)
    return pl.pallas_call(
        _sgu_kernel,
        grid=(t // rows,),
        in_specs=[
            pl.BlockSpec((rows, d), lambda i: (i, 0)),
            pl.BlockSpec((1, d), const),
            pl.BlockSpec((d, 2 * SGU_WIDTH), const, pipeline_mode=resident),
            pl.BlockSpec((1, SGU_WIDTH), const),
            pl.BlockSpec((1, SGU_WIDTH), const),
            pl.BlockSpec((SGU_GROUPS, CHUNK, CHUNK), lambda i: (0, 0, 0)),
            pl.BlockSpec((CHUNK, SGU_GROUPS), const),
            pl.BlockSpec((SGU_WIDTH, d), const, pipeline_mode=resident),
            pl.BlockSpec((1, d), const),
        ],
        out_specs=pl.BlockSpec((rows, d), lambda i: (i, 0)),
        out_shape=jax.ShapeDtypeStruct((t, d), F32),
        scratch_shapes=[
            pltpu.VMEM((rows, d), BF16),
            pltpu.VMEM((rows, SGU_WIDTH), F32),
            pltpu.VMEM((rows, SGU_WIDTH), BF16),
            pltpu.VMEM((rows, SGU_WIDTH), BF16),
        ],
        compiler_params=pltpu.CompilerParams(
            dimension_semantics=("parallel",),
            vmem_limit_bytes=VMEM_LIMIT_BYTES),
        name="chunked_sgu",
    )(x2d, gpre, w_in, ln_g, ln_b, w_s, b_st, w_out, gpost)


def _conv_table(conv_w, conv_b):
    pad = jnp.zeros((SUBLANES - 4, FFN_DIM), F32)
    return jnp.concatenate([conv_w, conv_b[None, :], pad], axis=0)


def _row(v):
    return v.reshape(1, -1)


def kernel(x, attn_w_qkv, attn_lambda_q1, attn_lambda_k1, attn_lambda_q2, attn_lambda_k2, attn_subln, attn_w_o, sgu_w_in, sgu_ln_g, sgu_ln_b, sgu_w_s, sgu_b_s, sgu_w_out, norm_mix_pre, norm_mix_post, norm_ffn_pre, norm_ffn_post, ffn_w_up, ffn_conv_w, ffn_conv_b, ffn_w_down):
    b, s, d = x.shape
    assert d == D_MODEL and s % max(QKV_ROWS, ROW_TILE, ATTN_TQ) == 0 and ATTN_TQ % ATTN_TK == 0

    w_qkv = attn_w_qkv[0]
    wqt = (w_qkv[:, :d].T * (HEAD_DIM ** -0.5 * LOG2E)).astype(BF16)
    wk = w_qkv[:, d:2 * d].astype(BF16)
    wvt = w_qkv[:, 2 * d:].T.astype(BF16)
    qt, k, vt = _qkv_call(x, _row(norm_mix_pre[0]), wqt, wk, wvt)
    attn = _attn_call(qt, k, vt, _row(attn_lambda_q1[0]), _row(attn_lambda_k1[0]),
                      _row(attn_lambda_q2[0]), _row(attn_lambda_k2[0]),
                      jnp.broadcast_to(attn_subln[0][:, None], (V_DIM, V_DIM)))
    x2d = x.reshape(b * s, d)
    x2d = _ffn_call(x2d, s, _row(norm_ffn_pre[0]), ffn_w_up[0].astype(BF16),
                    _conv_table(ffn_conv_w[0], ffn_conv_b[0]), ffn_w_down[0].astype(BF16),
                    _row(norm_ffn_post[0]),
                    mixer_proj=(attn.reshape(b * s, d), attn_w_o[0].astype(BF16),
                                _row(norm_mix_post[0])))

    x2d = _sgu_call(x2d, _row(norm_mix_pre[1]), sgu_w_in[0].astype(BF16), _row(sgu_ln_g[0]),
                    _row(sgu_ln_b[0]), sgu_w_s[0], sgu_b_s[0].T, sgu_w_out[0].astype(BF16),
                    _row(norm_mix_post[1]))
    x2d = _ffn_call(x2d, s, _row(norm_ffn_pre[1]), ffn_w_up[1].astype(BF16),
                    _conv_table(ffn_conv_w[1], ffn_conv_b[1]), ffn_w_down[1].astype(BF16),
                    _row(norm_ffn_post[1]))
    return x2d.reshape(b, s, d)
```
